```python
import jax, jax.numpy as jnp
from jax import lax
import numpy as np

D_MODEL = 1024
BATCH = 2
SEQ = 8192
DEPTH = 4

GRID_W = 64
CTX_LEN = 256
CHUNK = 128
Q_BLOCK = 128
EPS = 1e-6
ROPE_THETA = 10000.0
N_MOD = 6

A_HEADS = 4
A_HEAD_DIM = 64
A_WIDTH = A_HEADS * A_HEAD_DIM
ATT_Q_HEADS = 8
ATT_KV_HEADS = 2
ATT_HEAD_DIM = 64
ATT_REP = ATT_Q_HEADS // ATT_KV_HEADS
ATT_WIDTH = ATT_Q_HEADS * ATT_HEAD_DIM
ATT_KV_WIDTH = ATT_KV_HEADS * ATT_HEAD_DIM
ATT_SCALE = ATT_HEAD_DIM ** -0.5
ROPE_AXIS_DIM = ATT_HEAD_DIM // 2
ROPE_AXIS_FREQS = ROPE_AXIS_DIM // 2
C_WIDTH = 256
D_MIX = A_WIDTH + ATT_WIDTH + C_WIDTH
D_FF = 2816

OFF_AU = 0
OFF_AV = OFF_AU + A_WIDTH
OFF_Q = OFF_AV + A_WIDTH
OFF_K = OFF_Q + ATT_WIDTH
OFF_V = OFF_K + ATT_KV_WIDTH
OFF_CB = OFF_V + ATT_KV_WIDTH
OFF_CC = OFF_CB + C_WIDTH
OFF_CH = OFF_CC + C_WIDTH
D_IN = OFF_CH + C_WIDTH

kernel_name = "hybrid_parallel_heads_diffusion_trunk"


def rms_norm(x, g):
    xf = x.astype(jnp.float32)
    y = xf * lax.rsqrt(jnp.mean(xf * xf, axis=-1, keepdims=True) + EPS)
    return (y * g.astype(jnp.float32)).astype(x.dtype)


def layer_norm_plain(x):
    xf = x.astype(jnp.float32)
    mu = jnp.mean(xf, axis=-1, keepdims=True)
    xc = xf - mu
    var = jnp.mean(xc * xc, axis=-1, keepdims=True)
    return (xc * lax.rsqrt(var + EPS)).astype(x.dtype)


def modulate(h, shift, scale):
    return h * (1 + scale) + shift


def dwconv3(x, w):
    xp = jnp.pad(x, ((0, 0), (1, 1), (0, 0)))
    return xp[:, :-2] * w[0] + xp[:, 1:-1] * w[1] + xp[:, 2:] * w[2]


def axial_rope_tables(n):
    rows = n // GRID_W
    row = jnp.repeat(jnp.arange(rows), GRID_W).astype(jnp.float32)
    col = jnp.tile(jnp.arange(GRID_W), rows).astype(jnp.float32)
    inv = ROPE_THETA ** (-2.0 * jnp.arange(ROPE_AXIS_FREQS, dtype=jnp.float32) / ROPE_AXIS_DIM)
    ang = jnp.stack([row[:, None] * inv, col[:, None] * inv], axis=1)
    return jnp.cos(ang), jnp.sin(ang)


def apply_rope(x, cos, sin):
    b, n, h, d = x.shape
    xr = x.astype(jnp.float32).reshape(b, n, h, 2, 2, ROPE_AXIS_FREQS)
    x1, x2 = xr[..., 0, :], xr[..., 1, :]
    cs, sn = cos[None, :, None], sin[None, :, None]
    out = jnp.stack([x1 * cs - x2 * sn, x2 * cs + x1 * sn], axis=-2)
    return out.reshape(b, n, h, d).astype(x.dtype)


def chunk_gmlp(u, v, w_s, b_s):
    b, n, _ = u.shape
    u = jax.nn.gelu(u)
    v = layer_norm_plain(jax.nn.gelu(v).reshape(b, n // CHUNK, CHUNK, A_HEADS, A_HEAD_DIM))
    mixed = jnp.einsum('hpq,bcqhd->bcphd', w_s, v) + b_s.T[None, None, :, :, None]
    return u * mixed.reshape(b, n, A_WIDTH)


def short_gated_conv(proj, conv_w):
    return proj[..., OFF_CB:OFF_CC] * dwconv3(proj[..., OFF_CC:OFF_CH] * proj[..., OFF_CH:D_IN], conv_w)


def q_heads(proj, q_g):
    b, n, _ = proj.shape
    return rms_norm(proj[..., OFF_Q:OFF_K].reshape(b, n, ATT_Q_HEADS, ATT_HEAD_DIM), q_g)


def kv_heads(kv_cols, k_g):
    b, n, _ = kv_cols.shape
    k = rms_norm(kv_cols[..., :ATT_KV_WIDTH].reshape(b, n, ATT_KV_HEADS, ATT_HEAD_DIM), k_g)
    v = kv_cols[..., ATT_KV_WIDTH:].reshape(b, n, ATT_KV_HEADS, ATT_HEAD_DIM)
    return k, v


def latent_attention(q, k_lat, v_lat, k_ctx, v_ctx):
    b, n = q.shape[:2]
    keys = jnp.concatenate([k_ctx, k_lat], axis=1)
    vals = jnp.concatenate([v_ctx, v_lat], axis=1)
    nb = n // Q_BLOCK
    qb = q.reshape(b, nb, Q_BLOCK, ATT_KV_HEADS, ATT_REP, ATT_HEAD_DIM).transpose(1, 0, 2, 3, 4, 5)

    def block(qi):
        s = jnp.einsum('bqgrd,bkgd->bgrqk', qi, keys, preferred_element_type=jnp.float32) * ATT_SCALE
        p = jax.nn.softmax(s, axis=-1).astype(vals.dtype)
        return jnp.einsum('bgrqk,bkgd->bqgrd', p, vals)

    o = lax.map(block, qb)
    return o.transpose(1, 0, 2, 3, 4, 5).reshape(b, n, ATT_WIDTH)


def context_attention(q, k, v):
    b, m = q.shape[:2]
    qg = q.reshape(b, m, ATT_KV_HEADS, ATT_REP, ATT_HEAD_DIM)
    s = jnp.einsum('bqgrd,bkgd->bgrqk', qg, k, preferred_element_type=jnp.float32) * ATT_SCALE
    p = jax.nn.softmax(s, axis=-1).astype(v.dtype)
    return jnp.einsum('bgrqk,bkgd->bqgrd', p, v).reshape(b, m, ATT_WIDTH)


def conv_ffn(h, w_up, w_conv, w_down):
    up = dwconv3(h @ w_up, w_conv)
    a, g = jnp.split(up, 2, axis=-1)
    return (jax.nn.silu(g) * a) @ w_down


def setup_inputs(seed: int = 0) -> dict:
    key = jax.random.key(seed)
    ks = jax.random.split(key, 20)
    f32 = jnp.float32
    nrm = lambda k, shape, s: jax.random.normal(k, shape, f32) * s
    return {
        "x": nrm(ks[0], (BATCH, SEQ, D_MODEL), 1.0),
        "c": nrm(ks[1], (BATCH, D_MODEL), 1.0),
        "ctx": nrm(ks[2], (BATCH, CTX_LEN, D_MODEL), 1.0),
        "c_ctx": nrm(ks[3], (D_MODEL,), 1.0),
        "w_mod": nrm(ks[4], (DEPTH, D_MODEL, N_MOD * D_MODEL), 0.5 * D_MODEL ** -0.5),
        "b_mod": nrm(ks[5], (DEPTH, N_MOD * D_MODEL), 0.02),
        "norm1_g": 1.0 + nrm(ks[6], (DEPTH, D_MODEL), 0.02),
        "w_in": nrm(ks[7], (DEPTH, D_MODEL, D_IN), D_MODEL ** -0.5),
        "q_norm_g": 1.0 + nrm(ks[8], (DEPTH, ATT_HEAD_DIM), 0.02),
        "k_norm_g": 1.0 + nrm(ks[9], (DEPTH, ATT_HEAD_DIM), 0.02),
        "gmlp_w": nrm(ks[10], (DEPTH, A_HEADS, CHUNK, CHUNK), CHUNK ** -0.5),
        "gmlp_b": 1.0 + nrm(ks[11], (DEPTH, A_HEADS, CHUNK), 0.02),
        "conv_c_w": nrm(ks[12], (DEPTH, 3, C_WIDTH), 3 ** -0.5),
        "w_out": nrm(ks[13], (DEPTH, D_MIX, D_MODEL), D_MIX ** -0.5),
        "norm2_g": 1.0 + nrm(ks[14], (DEPTH, D_MODEL), 0.02),
        "ffn_up": nrm(ks[15], (DEPTH, D_MODEL, 2 * D_FF), D_MODEL ** -0.5),
        "ffn_conv_w": nrm(ks[16], (DEPTH, 3, 2 * D_FF), 3 ** -0.5),
        "ffn_down": nrm(ks[17], (DEPTH, D_FF, D_MODEL), D_FF ** -0.5),
        "final_g": 1.0 + nrm(ks[18], (D_MODEL,), 0.02),
    }


def reference(x, c, ctx, c_ctx, w_mod, b_mod, norm1_g, w_in, q_norm_g, k_norm_g, gmlp_w, gmlp_b,
              conv_c_w, w_out, norm2_g, ffn_up, ffn_conv_w, ffn_down, final_g):
    b, n, d = x.shape
    cos, sin = axial_rope_tables(n)
    silu_c = jax.nn.silu(c)
    silu_cc = jax.nn.silu(c_ctx)
    xc = ctx
    for l in range(DEPTH):
        last = l == DEPTH - 1
        mod = (silu_c @ w_mod[l] + b_mod[l]).reshape(b, N_MOD, 1, d)
        mod_c = (silu_cc @ w_mod[l] + b_mod[l]).reshape(N_MOD, d)
        sh1, sc1, g1, sh2, sc2, g2 = [mod[:, i] for i in range(N_MOD)]
        csh1, csc1, cg1, csh2, csc2, cg2 = [mod_c[i] for i in range(N_MOD)]

        hc = modulate(rms_norm(xc, norm1_g[l]), csh1, csc1)
        if last:
            k_c, v_c = kv_heads(hc @ w_in[l][:, OFF_K:OFF_CB], k_norm_g[l])
        else:
            proj_c = hc @ w_in[l]
            k_c, v_c = kv_heads(proj_c[..., OFF_K:OFF_CB], k_norm_g[l])
            a_c = chunk_gmlp(proj_c[..., OFF_AU:OFF_AV], proj_c[..., OFF_AV:OFF_Q], gmlp_w[l], gmlp_b[l])
            att_c = context_attention(q_heads(proj_c, q_norm_g[l]), k_c, v_c)
            cm_c = short_gated_conv(proj_c, conv_c_w[l])
            mix_c = jnp.concatenate([a_c, att_c, cm_c], axis=-1) @ w_out[l]

        h = modulate(rms_norm(x, norm1_g[l]), sh1, sc1)
        proj = h @ w_in[l]
        q = apply_rope(q_heads(proj, q_norm_g[l]), cos, sin)
        k, v = kv_heads(proj[..., OFF_K:OFF_CB], k_norm_g[l])
        k = apply_rope(k, cos, sin)
        att = latent_attention(q, k, v, k_c, v_c)
        a = chunk_gmlp(proj[..., OFF_AU:OFF_AV], proj[..., OFF_AV:OFF_Q], gmlp_w[l], gmlp_b[l])
        cm = short_gated_conv(proj, conv_c_w[l])
        x = x + g1 * (jnp.concatenate([a, att, cm], axis=-1) @ w_out[l])
        x = x + g2 * conv_ffn(modulate(rms_norm(x, norm2_g[l]), sh2, sc2), ffn_up[l], ffn_conv_w[l], ffn_down[l])

        if not last:
            xc = xc + cg1 * mix_c
            xc = xc + cg2 * conv_ffn(modulate(rms_norm(xc, norm2_g[l]), csh2, csc2), ffn_up[l], ffn_conv_w[l], ffn_down[l])
    return rms_norm(x, final_g)
```

```python
import functools

import jax
import jax.numpy as jnp
from jax import lax
from jax.experimental import pallas as pl
from jax.experimental.pallas import tpu as pltpu

F32 = jnp.float32
BF16 = jnp.bfloat16

EPS = 1e-6
ROPE_THETA = 10000.0
GRID_W = 64
CHUNK = 128
N_MOD = 6

A_HEADS = 4
A_HEAD_DIM = 64
A_WIDTH = A_HEADS * A_HEAD_DIM
ATT_Q_HEADS = 8
ATT_KV_HEADS = 2
ATT_HEAD_DIM = 64
ATT_REP = ATT_Q_HEADS // ATT_KV_HEADS
ATT_WIDTH = ATT_Q_HEADS * ATT_HEAD_DIM
ATT_KV_WIDTH = ATT_KV_HEADS * ATT_HEAD_DIM
ATT_SCALE = ATT_HEAD_DIM ** -0.5
ROPE_AXIS_FREQS = ATT_HEAD_DIM // 4
C_WIDTH = 256

OFF_AU = 0
OFF_AV = OFF_AU + A_WIDTH
OFF_Q = OFF_AV + A_WIDTH
OFF_K = OFF_Q + ATT_WIDTH
OFF_V = OFF_K + ATT_KV_WIDTH
OFF_CB = OFF_V + ATT_KV_WIDTH
OFF_CC = OFF_CB + C_WIDTH
OFF_CH = OFF_CC + C_WIDTH
D_IN = OFF_CH + C_WIDTH

V7X_SUBLANES = 8
V7X_LANES = 128
V7X_VMEM_BYTES = 64 * 1024 * 1024
HALO = V7X_SUBLANES
SOFTMAX_ROWS = 80
NEG_BIG = -1e30


def _vmem_limit(nbytes):
    return int(min(max(nbytes, 16 * 1024 * 1024), V7X_VMEM_BYTES - 6 * 1024 * 1024))


def _sigmoid(x):
    return 1.0 / (1.0 + jnp.exp(-x))


def _gelu_tanh(x):
    c = 0.7978845608028654
    return x * (0.5 * (1.0 + jnp.tanh(c * (x + 0.044715 * (x * x * x)))))


def _split_dot(x, j):
    hi = x.astype(BF16)
    lo = (x - hi.astype(F32)).astype(BF16)
    return (jnp.dot(hi, j, preferred_element_type=F32)
            + jnp.dot(lo, j, preferred_element_type=F32))


def _norm_modulate(x, g, shift, scale):
    ms = jnp.mean(x * x, axis=-1, keepdims=True)
    return (x * lax.rsqrt(ms + EPS) * g) * (1.0 + scale) + shift


def _conv3_rows(u, prev_row, next_row, w3):
    rows = u.shape[0]
    rid = lax.broadcasted_iota(jnp.int32, u.shape, 0)
    dn = jnp.where(rid == 0, prev_row, pltpu.roll(u, 1, 0))
    up = jnp.where(rid == rows - 1, next_row, pltpu.roll(u, rows - 1, 0))
    return dn * w3[0:1, :] + u * w3[1:2, :] + up * w3[2:3, :]


def _rope128(x, cos_t, sin_t, odd16):
    sw = jnp.where(odd16, pltpu.roll(x, 16, 1), pltpu.roll(x, V7X_LANES - 16, 1))
    return x * cos_t + sw * sin_t


def _mod_kernel(c_ref, w_ref, b_ref, o_ref):
    c = c_ref[...]
    s = (c * _sigmoid(c)).astype(BF16)
    o_ref[0] = jnp.dot(s, w_ref[0].astype(BF16), preferred_element_type=F32) + b_ref[0]


def _mod_call(c_rows, w_mod, b_mod):
    depth, d, nm = w_mod.shape
    bn = 1536
    return pl.pallas_call(
        _mod_kernel,
        grid=(depth, nm // bn),
        in_specs=[
            pl.BlockSpec((V7X_SUBLANES, d), lambda l, j: (0, 0)),
            pl.BlockSpec((1, d, bn), lambda l, j: (l, 0, j)),
            pl.BlockSpec((1, 1, bn), lambda l, j: (l, 0, j)),
        ],
        out_specs=pl.BlockSpec((1, V7X_SUBLANES, bn), lambda l, j: (l, 0, j)),
        out_shape=jax.ShapeDtypeStruct((depth, V7X_SUBLANES, nm), F32),
        compiler_params=pltpu.CompilerParams(
            dimension_semantics=("arbitrary", "arbitrary"),
            vmem_limit_bytes=_vmem_limit(3 * d * bn * 4)),
        name="adaln_mod",
    )(c_rows, w_mod, b_mod.reshape(depth, 1, nm))


def _inproj_kernel(x_ref, xp_ref, xn_ref, mod_ref, g_ref, w_ref, qg_ref, kg_ref,
                   cos_ref, sin_ref, j_ref, gw_ref, gb_ref, cw_ref,
                   q_ref, k_ref, v_ref, a_ref, cm_ref, *, tm):
    i = pl.program_id(1)
    last = pl.num_programs(1) - 1
    pm = (i > 0).astype(F32)
    nm = (i < last).astype(F32)

    xa = jnp.concatenate([x_ref[0], xp_ref[0], xn_ref[0]], axis=0)
    mod = mod_ref[0]
    h = _norm_modulate(xa, g_ref[...], mod[0:1, :], mod[1:2, :]).astype(BF16)
    proj = jnp.dot(h[:tm], w_ref[:, :OFF_CC], preferred_element_type=F32)
    pch = jnp.dot(h, w_ref[:, OFF_CC:], preferred_element_type=F32)

    cos_t = cos_ref[...]
    sin_t = sin_ref[...]
    lane = lax.broadcasted_iota(jnp.int32, (tm, V7X_LANES), 1)
    odd16 = (lane & 16) != 0
    inv_hd = 1.0 / ATT_HEAD_DIM

    q = proj[:, OFF_Q:OFF_K]
    qn = q * lax.rsqrt(_split_dot(q * q, j_ref[...]) * inv_hd + EPS) * qg_ref[...]
    for t in range(ATT_WIDTH // V7X_LANES):
        blk = _rope128(qn[:, t * V7X_LANES:(t + 1) * V7X_LANES], cos_t, sin_t, odd16)
        q_ref[0, :, t * V7X_LANES:(t + 1) * V7X_LANES] = (blk * ATT_SCALE).astype(BF16)
    k = proj[:, OFF_K:OFF_V]
    kn = k * lax.rsqrt(_split_dot(k * k, j_ref[:ATT_KV_WIDTH, :ATT_KV_WIDTH]) * inv_hd + EPS) * kg_ref[...]
    k_ref[0] = _rope128(kn, cos_t, sin_t, odd16).astype(BF16)
    v_ref[0] = proj[:, OFF_V:OFF_CB].astype(BF16)

    u_act = _gelu_tanh(proj[:, OFF_AU:OFF_AV])
    v_act = _gelu_tanh(proj[:, OFF_AV:OFF_Q])
    jl = j_ref[:A_WIDTH, :A_WIDTH]
    mu = _split_dot(v_act, jl) * (1.0 / A_HEAD_DIM)
    xc = v_act - mu
    var = _split_dot(xc * xc, jl) * (1.0 / A_HEAD_DIM)
    vln = xc * lax.rsqrt(var + EPS)
    lane_a = lax.broadcasted_iota(jnp.int32, (CHUNK, A_WIDTH), 1)
    for c in range(tm // CHUNK):
        vc = vln[c * CHUNK:(c + 1) * CHUNK]
        vs = jnp.concatenate(
            [jnp.where((lane_a // A_HEAD_DIM) == hh, vc, 0.0).astype(BF16) for hh in range(A_HEADS)],
            axis=0)
        mixed = jnp.dot(gw_ref[...], vs, preferred_element_type=F32) + gb_ref[...]
        a_ref[0, c * CHUNK:(c + 1) * CHUNK, :] = (u_act[c * CHUNK:(c + 1) * CHUNK] * mixed).astype(BF16)

    ch = pch[:, :C_WIDTH] * pch[:, C_WIDTH:]
    prev_row = ch[tm + HALO - 1:tm + HALO] * pm
    next_row = ch[tm + HALO:tm + HALO + 1] * nm
    cm = proj[:, OFF_CB:OFF_CC] * _conv3_rows(ch[:tm], prev_row, next_row, cw_ref[...])
    cm_ref[0] = cm.astype(BF16)


def _row_tile_specs(tm, n, d):
    nb8 = n // HALO
    r = tm // HALO
    return [
        pl.BlockSpec((1, tm, d), lambda b, i: (b, i, 0)),
        pl.BlockSpec((1, HALO, d), lambda b, i: (b, jnp.maximum(i * r - 1, 0), 0)),
        pl.BlockSpec((1, HALO, d), lambda b, i: (b, jnp.minimum((i + 1) * r, nb8 - 1), 0)),
    ]


def _const_spec(shape):
    nd = len(shape)
    return pl.BlockSpec(shape, lambda b, i: (0,) * nd, pipeline_mode=pl.Buffered(1))


def _inproj_call(x, mod, g1, w_in, qg, kg, cos_t, sin_t, jmat, gw, gb, cw, tm):
    bsz, n, d = x.shape
    kern = functools.partial(_inproj_kernel, tm=tm)
    row = lambda w: pl.BlockSpec((1, tm, w), lambda b, i: (b, i, 0))
    est = (4 * tm * d * 4 + w_in.size * 2 + 8 * tm * D_IN * 4 + 4 * tm * 2048 * 2)
    return pl.pallas_call(
        kern,
        grid=(bsz, n // tm),
        in_specs=_row_tile_specs(tm, n, d) + [
            pl.BlockSpec((1, N_MOD, d), lambda b, i: (b, 0, 0)),
            _const_spec((1, d)),
            _const_spec(w_in.shape),
            _const_spec(qg.shape),
            _const_spec(kg.shape),
            pl.BlockSpec((tm, V7X_LANES), lambda b, i: (i, 0)),
            pl.BlockSpec((tm, V7X_LANES), lambda b, i: (i, 0)),
            _const_spec(jmat.shape),
            _const_spec(gw.shape),
            _const_spec(gb.shape),
            _const_spec(cw.shape),
        ],
        out_specs=[row(ATT_WIDTH), row(ATT_KV_WIDTH), row(ATT_KV_WIDTH), row(A_WIDTH), row(C_WIDTH)],
        out_shape=[
            jax.ShapeDtypeStruct((bsz, n, ATT_WIDTH), BF16),
            jax.ShapeDtypeStruct((bsz, n, ATT_KV_WIDTH), BF16),
            jax.ShapeDtypeStruct((bsz, n, ATT_KV_WIDTH), BF16),
            jax.ShapeDtypeStruct((bsz, n, A_WIDTH), BF16),
            jax.ShapeDtypeStruct((bsz, n, C_WIDTH), BF16),
        ],
        compiler_params=pltpu.CompilerParams(
            dimension_semantics=("arbitrary", "arbitrary"),
            vmem_limit_bytes=_vmem_limit(est)),
        name="inproj",
    )(x, x, x, mod, g1, w_in, qg, kg, cos_t, sin_t, jmat, gw, gb, cw)


def _attn_kernel(qt_ref, k_ref, vt_ref, o_ref, *, nkb):
    qt = qt_ref[0, 0]
    tq = qt.shape[1]

    def step(j, carry):
        m, acc = carry
        st = jnp.dot(k_ref[0, 0, j], qt, preferred_element_type=F32)
        m_new = jnp.maximum(m, jnp.max(st, axis=0, keepdims=True))
        p = jnp.exp(st - m_new).astype(BF16)
        alpha = jnp.exp(m - m_new)
        acc = alpha * acc + jnp.dot(vt_ref[0, 0, j], p, preferred_element_type=F32)
        return m_new, acc

    m0 = jnp.full((1, tq), NEG_BIG, F32)
    acc0 = jnp.zeros((SOFTMAX_ROWS, tq), F32)
    _, acc = lax.fori_loop(0, nkb, step, (m0, acc0))
    o_ref[0, 0] = (acc[:ATT_HEAD_DIM] / acc[ATT_HEAD_DIM:ATT_HEAD_DIM + 1]).astype(o_ref.dtype)


def _attn_call(qt, kb, vtb, tq):
    bsz, g, hd, nq = qt.shape
    nkb, tk = kb.shape[2], kb.shape[3]
    kern = functools.partial(_attn_kernel, nkb=nkb)
    est = 2 * (kb.size // (bsz * g)) * 2 * 2 + 2 * (vtb.size // (bsz * g)) * 2 + 6 * tk * tq * 4
    return pl.pallas_call(
        kern,
        grid=(bsz, g, nq // tq),
        in_specs=[
            pl.BlockSpec((1, 1, hd, tq), lambda b, gg, i: (b, gg, 0, i)),
            pl.BlockSpec((1, 1) + kb.shape[2:], lambda b, gg, i: (b, gg, 0, 0, 0)),
            pl.BlockSpec((1, 1) + vtb.shape[2:], lambda b, gg, i: (b, gg, 0, 0, 0)),
        ],
        out_specs=pl.BlockSpec((1, 1, hd, tq), lambda b, gg, i: (b, gg, 0, i)),
        out_shape=jax.ShapeDtypeStruct((bsz, g, hd, nq), BF16),
        compiler_params=pltpu.CompilerParams(
            dimension_semantics=("arbitrary", "arbitrary", "arbitrary"),
            vmem_limit_bytes=_vmem_limit(est)),
        name="flash_attn",
    )(qt, kb, vtb)


def _key_block(nk):
    for tk in (768, 512, 256, 128):
        if nk % tk == 0:
            return tk
    raise ValueError(f"unsupported key count {nk}")


def _attention(q, k_all, v_all, tq):
    bsz, n, _ = q.shape
    nk = k_all.shape[1]
    tk = _key_block(nk)
    g, r, hd = ATT_KV_HEADS, ATT_REP, ATT_HEAD_DIM
    qt = q.reshape(bsz, n, g, r, hd).transpose(0, 2, 4, 3, 1).reshape(bsz, g, hd, r * n)
    kb = k_all.reshape(bsz, nk, g, hd).transpose(0, 2, 1, 3).reshape(bsz, g, nk // tk, tk, hd)
    vt = v_all.reshape(bsz, nk, g, hd).transpose(0, 2, 3, 1)
    ones = jnp.ones((bsz, g, 1, nk), BF16)
    pad = jnp.zeros((bsz, g, SOFTMAX_ROWS - hd - 1, nk), BF16)
    vtb = jnp.concatenate([vt, ones, pad], axis=2)
    vtb = vtb.reshape(bsz, g, SOFTMAX_ROWS, nk // tk, tk).transpose(0, 1, 3, 2, 4)
    ot = _attn_call(qt, kb, vtb, tq)
    return ot.reshape(bsz, g, hd, r, n).transpose(0, 4, 1, 3, 2).reshape(bsz, n, ATT_WIDTH)


def _outproj_kernel(x_ref, a_ref, att_ref, cm_ref, mod_ref, w_ref, o_ref):
    mix = (jnp.dot(a_ref[0], w_ref[:A_WIDTH, :], preferred_element_type=F32)
           + jnp.dot(att_ref[0], w_ref[A_WIDTH:A_WIDTH + ATT_WIDTH, :], preferred_element_type=F32)
           + jnp.dot(cm_ref[0], w_ref[A_WIDTH + ATT_WIDTH:, :], preferred_element_type=F32))
    o_ref[0] = x_ref[0] + mod_ref[0][2:3, :] * mix


def _outproj_call(x, a, att, cm, mod, w_out, tm):
    bsz, n, d = x.shape
    row = lambda w: pl.BlockSpec((1, tm, w), lambda b, i: (b, i, 0))
    est = 6 * tm * d * 4 + w_out.size * 2 + 4 * tm * d * 2
    return pl.pallas_call(
        _outproj_kernel,
        grid=(bsz, n // tm),
        in_specs=[row(d), row(A_WIDTH), row(ATT_WIDTH), row(C_WIDTH),
                  pl.BlockSpec((1, N_MOD, d), lambda b, i: (b, 0, 0)),
                  _const_spec(w_out.shape)],
        out_specs=row(d),
        out_shape=jax.ShapeDtypeStruct((bsz, n, d), F32),
        compiler_params=pltpu.CompilerParams(
            dimension_semantics=("arbitrary", "arbitrary"),
            vmem_limit_bytes=_vmem_limit(est)),
        name="outproj",
    )(x, a, att, cm, mod, w_out)


def _ffn_kernel(x_ref, xp_ref, xn_ref, mod_ref, g_ref, wu_ref, wc_ref, wd_ref, fg_ref,
                o_ref, acc_ref, *, tm, nchunk, cf, final_norm):
    i = pl.program_id(1)
    last = pl.num_programs(1) - 1
    pm = (i > 0).astype(F32)
    nm = (i < last).astype(F32)

    x = x_ref[0]
    xa = jnp.concatenate([x, xp_ref[0], xn_ref[0]], axis=0)
    mod = mod_ref[0]
    h = _norm_modulate(xa, g_ref[...], mod[3:4, :], mod[4:5, :]).astype(BF16)
    for j in range(nchunk):
        up = jnp.dot(h, wu_ref[:, j * 2 * cf:(j + 1) * 2 * cf], preferred_element_type=F32)
        prev_row = up[tm + HALO - 1:tm + HALO] * pm
        next_row = up[tm + HALO:tm + HALO + 1] * nm
        cv = _conv3_rows(up[:tm], prev_row, next_row, wc_ref[:, j * 2 * cf:(j + 1) * 2 * cf])
        a_ = cv[:, :cf]
        g_ = cv[:, cf:]
        act = ((g_ * _sigmoid(g_)) * a_).astype(BF16)
        part = jnp.dot(act, wd_ref[j * cf:(j + 1) * cf, :], preferred_element_type=F32)
        if j == 0:
            acc_ref[...] = part
        else:
            acc_ref[...] += part
    y = x + mod[5:6, :] * acc_ref[...]
    if final_norm:
        ms = jnp.mean(y * y, axis=-1, keepdims=True)
        y = y * lax.rsqrt(ms + EPS) * fg_ref[...]
    o_ref[0] = y


def _ffn_call(x, mod, g2, wu, wc, wd, fg, tm, final_norm):
    bsz, n, d = x.shape
    d_ff = wd.shape[0]
    cf = 256
    nchunk = d_ff // cf
    kern = functools.partial(_ffn_kernel, tm=tm, nchunk=nchunk, cf=cf, final_norm=final_norm)
    est = (wu.size + wd.size) * 2 + 10 * tm * d * 4 + 8 * tm * 2 * cf * 4
    return pl.pallas_call(
        kern,
        grid=(bsz, n // tm),
        in_specs=_row_tile_specs(tm, n, d) + [
            pl.BlockSpec((1, N_MOD, d), lambda b, i: (b, 0, 0)),
            _const_spec((1, d)),
            _const_spec(wu.shape),
            _const_spec(wc.shape),
            _const_spec(wd.shape),
            _const_spec((1, d)),
        ],
        out_specs=pl.BlockSpec((1, tm, d), lambda b, i: (b, i, 0)),
        out_shape=jax.ShapeDtypeStruct((bsz, n, d), F32),
        scratch_shapes=[pltpu.VMEM((tm, d), F32)],
        compiler_params=pltpu.CompilerParams(
            dimension_semantics=("arbitrary", "arbitrary"),
            vmem_limit_bytes=_vmem_limit(est)),
        name="convffn",
    )(x, x, x, mod, g2, wu, wc, wd, fg)


def _rope_tables(n):
    rows = n // GRID_W
    row = jnp.repeat(jnp.arange(rows), GRID_W).astype(F32)
    col = jnp.tile(jnp.arange(GRID_W), rows).astype(F32)
    inv = ROPE_THETA ** (-2.0 * jnp.arange(ROPE_AXIS_FREQS, dtype=F32) / (2 * ROPE_AXIS_FREQS))
    ar = row[:, None] * inv
    ac = col[:, None] * inv
    cos_h = jnp.concatenate([jnp.cos(ar), jnp.cos(ar), jnp.cos(ac), jnp.cos(ac)], axis=1)
    sin_h = jnp.concatenate([-jnp.sin(ar), jnp.sin(ar), -jnp.sin(ac), jnp.sin(ac)], axis=1)
    return jnp.tile(cos_h, (1, 2)), jnp.tile(sin_h, (1, 2))


def _block_ones(width, block):
    idx = jnp.arange(width) // block
    return (idx[:, None] == idx[None, :]).astype(BF16)


def _interleave_gate(w, d_ff, cf):
    lead = w.shape[:-1]
    a = w[..., :d_ff].reshape(lead + (d_ff // cf, cf))
    g = w[..., d_ff:].reshape(lead + (d_ff // cf, cf))
    return jnp.concatenate([a, g], axis=-1).reshape(lead + (2 * d_ff,))


def kernel(x, c, ctx, c_ctx, w_mod, b_mod, norm1_g, w_in, q_norm_g, k_norm_g, gmlp_w, gmlp_b,
           conv_c_w, w_out, norm2_g, ffn_up, ffn_conv_w, ffn_down, final_g):
    bsz, n, d = x.shape
    n_ctx = ctx.shape[1]
    depth = w_mod.shape[0]
    d_ff = ffn_down.shape[1]
    assert n % 512 == 0 and n_ctx % CHUNK == 0 and bsz + 1 <= V7X_SUBLANES

    tm_lat, tm_ctx = 512, n_ctx
    tq = 512

    c_rows = jnp.concatenate(
        [c, c_ctx[None, :], jnp.zeros((V7X_SUBLANES - bsz - 1, d), F32)], axis=0)
    mod_all = _mod_call(c_rows, w_mod, b_mod)

    cos_l, sin_l = _rope_tables(n)
    cos_c = jnp.ones((n_ctx, V7X_LANES), F32)
    sin_c = jnp.zeros((n_ctx, V7X_LANES), F32)
    jmat = _block_ones(ATT_WIDTH, ATT_HEAD_DIM)
    final_row = final_g.reshape(1, d)

    xc = ctx
    for l in range(depth):
        last = l == depth - 1
        mod_lat = mod_all[l, :bsz].reshape(bsz, N_MOD, d)
        mod_ctx = jnp.broadcast_to(mod_all[l, bsz].reshape(1, N_MOD, d), (bsz, N_MOD, d))
        w_in_b = w_in[l].astype(BF16)
        w_out_b = w_out[l].astype(BF16)
        wu = _interleave_gate(ffn_up[l], d_ff, 256).astype(BF16)
        wc = _interleave_gate(ffn_conv_w[l], d_ff, 256)
        wd = ffn_down[l].astype(BF16)
        g1 = norm1_g[l].reshape(1, d)
        g2 = norm2_g[l].reshape(1, d)
        qg = jnp.tile(q_norm_g[l], ATT_Q_HEADS).reshape(1, ATT_WIDTH)
        kg = jnp.tile(k_norm_g[l], ATT_KV_HEADS).reshape(1, ATT_KV_WIDTH)
        gw = gmlp_w[l].transpose(1, 0, 2).reshape(CHUNK, A_HEADS * CHUNK).astype(BF16)
        gb = jnp.repeat(gmlp_b[l].T, A_HEAD_DIM, axis=1)
        cw = conv_c_w[l]
        shared = (w_in_b, qg, kg)
        tail = (jmat, gw, gb, cw)

        q_c, k_c, v_c, a_c, cm_c = _inproj_call(
            xc, mod_ctx, g1, *shared, cos_c, sin_c, *tail, tm=tm_ctx)
        if not last:
            att_c = _attention(q_c, k_c, v_c, tq)
            xc = _outproj_call(xc, a_c, att_c, cm_c, mod_ctx, w_out_b, tm_ctx)
            xc = _ffn_call(xc, mod_ctx, g2, wu, wc, wd, final_row, tm_ctx, False)

        q_l, k_l, v_l, a_l, cm_l = _inproj_call(
            x, mod_lat, g1, *shared, cos_l, sin_l, *tail, tm=tm_lat)
        k_all = jnp.concatenate([k_c, k_l], axis=1)
        v_all = jnp.concatenate([v_c, v_l], axis=1)
        att = _attention(q_l, k_all, v_all, tq)
        x = _outproj_call(x, a_l, att, cm_l, mod_lat, w_out_b, 1024)
        x = _ffn_call(x, mod_lat, g2, wu, wc, wd, final_row, tm_lat, last)
    return x
```

```python
import functools

import jax
import jax.numpy as jnp
from jax import lax
from jax.experimental import pallas as pl
from jax.experimental.pallas import tpu as pltpu

F32 = jnp.float32
BF16 = jnp.bfloat16

EPS = 1e-6
ROPE_THETA = 10000.0
GRID_W = 64
CHUNK = 128
N_MOD = 6

A_HEADS = 4
A_HEAD_DIM = 64
A_WIDTH = A_HEADS * A_HEAD_DIM
ATT_Q_HEADS = 8
ATT_KV_HEADS = 2
ATT_HEAD_DIM = 64
ATT_REP = ATT_Q_HEADS // ATT_KV_HEADS
ATT_WIDTH = ATT_Q_HEADS * ATT_HEAD_DIM
ATT_KV_WIDTH = ATT_KV_HEADS * ATT_HEAD_DIM
ATT_SCALE = ATT_HEAD_DIM ** -0.5
LOG2_E = 1.4426950408889634
ROPE_AXIS_FREQS = ATT_HEAD_DIM // 4
C_WIDTH = 256

OFF_AU = 0
OFF_AV = OFF_AU + A_WIDTH
OFF_Q = OFF_AV + A_WIDTH
OFF_K = OFF_Q + ATT_WIDTH
OFF_V = OFF_K + ATT_KV_WIDTH
OFF_CB = OFF_V + ATT_KV_WIDTH
OFF_CC = OFF_CB + C_WIDTH
OFF_CH = OFF_CC + C_WIDTH
D_IN = OFF_CH + C_WIDTH

V7X_SUBLANES = 8
V7X_LANES = 128
V7X_VMEM_BYTES = 64 * 1024 * 1024
HALO = V7X_SUBLANES
SOFTMAX_ROWS = 80
NEG_BIG = -1e30
SCORE_BOUND_LOG2 = 50.0


def _vmem_limit(nbytes):
    return int(min(max(nbytes, 16 * 1024 * 1024), V7X_VMEM_BYTES - 6 * 1024 * 1024))


def _sigmoid(x):
    return 1.0 / (1.0 + jnp.exp(-x))


def _gelu_tanh(x):
    c = 0.7978845608028654
    return x * (0.5 * (1.0 + jnp.tanh(c * (x + 0.044715 * (x * x * x)))))


def _split_dot(x, j):
    hi = x.astype(BF16)
    lo = (x - hi.astype(F32)).astype(BF16)
    return (jnp.dot(hi, j, preferred_element_type=F32)
            + jnp.dot(lo, j, preferred_element_type=F32))


def _norm_modulate(x, g, shift, scale):
    ms = jnp.mean(x * x, axis=-1, keepdims=True)
    return (x * lax.rsqrt(ms + EPS) * g) * (1.0 + scale) + shift


def _conv3_rows(u, prev_row, next_row, w3):
    rows = u.shape[0]
    rid = lax.broadcasted_iota(jnp.int32, u.shape, 0)
    dn = jnp.where(rid == 0, prev_row, pltpu.roll(u, 1, 0))
    up = jnp.where(rid == rows - 1, next_row, pltpu.roll(u, rows - 1, 0))
    return dn * w3[0:1, :] + u * w3[1:2, :] + up * w3[2:3, :]


def _rope128(x, cos_t, sin_t, odd16):
    sw = jnp.where(odd16, pltpu.roll(x, 16, 1), pltpu.roll(x, V7X_LANES - 16, 1))
    return x * cos_t + sw * sin_t


def _mod_kernel(c_ref, w_ref, b_ref, o_ref):
    c = c_ref[...]
    s = (c * _sigmoid(c)).astype(BF16)
    o_ref[0] = jnp.dot(s, w_ref[0].astype(BF16), preferred_element_type=F32) + b_ref[0]


def _mod_call(c_rows, w_mod, b_mod):
    depth, d, nm = w_mod.shape
    bn = 1536
    return pl.pallas_call(
        _mod_kernel,
        grid=(depth, nm // bn),
        in_specs=[
            pl.BlockSpec((V7X_SUBLANES, d), lambda l, j: (0, 0)),
            pl.BlockSpec((1, d, bn), lambda l, j: (l, 0, j)),
            pl.BlockSpec((1, 1, bn), lambda l, j: (l, 0, j)),
        ],
        out_specs=pl.BlockSpec((1, V7X_SUBLANES, bn), lambda l, j: (l, 0, j)),
        out_shape=jax.ShapeDtypeStruct((depth, V7X_SUBLANES, nm), F32),
        compiler_params=pltpu.CompilerParams(
            dimension_semantics=("arbitrary", "arbitrary"),
            vmem_limit_bytes=_vmem_limit(3 * d * bn * 4)),
        name="adaln_mod",
    )(c_rows, w_mod, b_mod.reshape(depth, 1, nm))


def _inproj_kernel(x_ref, xp_ref, xn_ref, mod_ref, g_ref, w_ref, qg_ref, kg_ref,
                   cos_ref, sin_ref, j_ref, gw_ref, gb_ref, cw_ref,
                   q_ref, k_ref, v_ref, a_ref, cm_ref, *, tm):
    i = pl.program_id(1)
    last = pl.num_programs(1) - 1
    pm = (i > 0).astype(F32)
    nm = (i < last).astype(F32)

    xa = jnp.concatenate([x_ref[0], xp_ref[0], xn_ref[0]], axis=0)
    mod = mod_ref[0]
    h = _norm_modulate(xa, g_ref[...], mod[0:1, :], mod[1:2, :]).astype(BF16)
    proj = jnp.dot(h[:tm], w_ref[:, :OFF_CC], preferred_element_type=F32)
    pch = jnp.dot(h, w_ref[:, OFF_CC:], preferred_element_type=F32)

    cos_t = cos_ref[...]
    sin_t = sin_ref[...]
    lane = lax.broadcasted_iota(jnp.int32, (tm, V7X_LANES), 1)
    odd16 = (lane & 16) != 0
    inv_hd = 1.0 / ATT_HEAD_DIM

    q = proj[:, OFF_Q:OFF_K]
    qn = q * lax.rsqrt(_split_dot(q * q, j_ref[...]) * inv_hd + EPS) * qg_ref[...]
    for t in range(ATT_WIDTH // V7X_LANES):
        blk = _rope128(qn[:, t * V7X_LANES:(t + 1) * V7X_LANES], cos_t, sin_t, odd16)
        q_ref[0, :, t * V7X_LANES:(t + 1) * V7X_LANES] = (blk * (ATT_SCALE * LOG2_E)).astype(BF16)
    k = proj[:, OFF_K:OFF_V]
    kn = k * lax.rsqrt(_split_dot(k * k, j_ref[:ATT_KV_WIDTH, :ATT_KV_WIDTH]) * inv_hd + EPS) * kg_ref[...]
    k_ref[0] = _rope128(kn, cos_t, sin_t, odd16).astype(BF16)
    v_ref[0] = proj[:, OFF_V:OFF_CB].astype(BF16)

    u_act = _gelu_tanh(proj[:, OFF_AU:OFF_AV])
    v_act = _gelu_tanh(proj[:, OFF_AV:OFF_Q])
    jl = j_ref[:A_WIDTH, :A_WIDTH]
    mu = _split_dot(v_act, jl) * (1.0 / A_HEAD_DIM)
    xc = v_act - mu
    var = _split_dot(xc * xc, jl) * (1.0 / A_HEAD_DIM)
    vln = xc * lax.rsqrt(var + EPS)
    lane_a = lax.broadcasted_iota(jnp.int32, (CHUNK, A_WIDTH), 1)
    for c in range(tm // CHUNK):
        vc = vln[c * CHUNK:(c + 1) * CHUNK]
        vs = jnp.concatenate(
            [jnp.where((lane_a // A_HEAD_DIM) == hh, vc, 0.0).astype(BF16) for hh in range(A_HEADS)],
            axis=0)
        mixed = jnp.dot(gw_ref[...], vs, preferred_element_type=F32) + gb_ref[...]
        a_ref[0, c * CHUNK:(c + 1) * CHUNK, :] = (u_act[c * CHUNK:(c + 1) * CHUNK] * mixed).astype(BF16)

    ch = pch[:, :C_WIDTH] * pch[:, C_WIDTH:]
    prev_row = ch[tm + HALO - 1:tm + HALO] * pm
    next_row = ch[tm + HALO:tm + HALO + 1] * nm
    cm = proj[:, OFF_CB:OFF_CC] * _conv3_rows(ch[:tm], prev_row, next_row, cw_ref[...])
    cm_ref[0] = cm.astype(BF16)


def _row_tile_specs(tm, n, d):
    nb8 = n // HALO
    r = tm // HALO
    return [
        pl.BlockSpec((1, tm, d), lambda b, i: (b, i, 0)),
        pl.BlockSpec((1, HALO, d), lambda b, i: (b, jnp.maximum(i * r - 1, 0), 0)),
        pl.BlockSpec((1, HALO, d), lambda b, i: (b, jnp.minimum((i + 1) * r, nb8 - 1), 0)),
    ]


def _const_spec(shape):
    nd = len(shape)
    return pl.BlockSpec(shape, lambda b, i: (0,) * nd, pipeline_mode=pl.Buffered(1))


def _inproj_call(x, mod, g1, w_in, qg, kg, cos_t, sin_t, jmat, gw, gb, cw, tm):
    bsz, n, d = x.shape
    kern = functools.partial(_inproj_kernel, tm=tm)
    row = lambda w: pl.BlockSpec((1, tm, w), lambda b, i: (b, i, 0))
    est = (4 * tm * d * 4 + w_in.size * 2 + 8 * tm * D_IN * 4 + 4 * tm * 2048 * 2)
    return pl.pallas_call(
        kern,
        grid=(bsz, n // tm),
        in_specs=_row_tile_specs(tm, n, d) + [
            pl.BlockSpec((1, N_MOD, d), lambda b, i: (b, 0, 0)),
            _const_spec((1, d)),
            _const_spec(w_in.shape),
            _const_spec(qg.shape),
            _const_spec(kg.shape),
            pl.BlockSpec((tm, V7X_LANES), lambda b, i: (i, 0)),
            pl.BlockSpec((tm, V7X_LANES), lambda b, i: (i, 0)),
            _const_spec(jmat.shape),
            _const_spec(gw.shape),
            _const_spec(gb.shape),
            _const_spec(cw.shape),
        ],
        out_specs=[row(ATT_WIDTH), row(ATT_KV_WIDTH), row(ATT_KV_WIDTH), row(A_WIDTH), row(C_WIDTH)],
        out_shape=[
            jax.ShapeDtypeStruct((bsz, n, ATT_WIDTH), BF16),
            jax.ShapeDtypeStruct((bsz, n, ATT_KV_WIDTH), BF16),
            jax.ShapeDtypeStruct((bsz, n, ATT_KV_WIDTH), BF16),
            jax.ShapeDtypeStruct((bsz, n, A_WIDTH), BF16),
            jax.ShapeDtypeStruct((bsz, n, C_WIDTH), BF16),
        ],
        compiler_params=pltpu.CompilerParams(
            dimension_semantics=("arbitrary", "arbitrary"),
            vmem_limit_bytes=_vmem_limit(est)),
        name="inproj",
    )(x, x, x, mod, g1, w_in, qg, kg, cos_t, sin_t, jmat, gw, gb, cw)


def _attn_bounded_kernel(qt_ref, k_ref, vt_ref, o_ref, *, nkb):
    qt = qt_ref[0, 0]
    acc = None
    st_next = jnp.dot(k_ref[0, 0, 0], qt, preferred_element_type=F32)
    for j in range(nkb):
        st = st_next
        if j + 1 < nkb:
            st_next = jnp.dot(k_ref[0, 0, j + 1], qt, preferred_element_type=F32)
        part = jnp.dot(vt_ref[0, 0, j], jnp.exp2(st).astype(BF16), preferred_element_type=F32)
        acc = part if acc is None else acc + part
    o_ref[0, 0] = (acc[:ATT_HEAD_DIM] / acc[ATT_HEAD_DIM:ATT_HEAD_DIM + 1]).astype(o_ref.dtype)


def _attn_online_kernel(qt_ref, k_ref, vt_ref, o_ref, *, nkb):
    qt = qt_ref[0, 0]
    tq = qt.shape[1]

    def step(j, carry):
        m, acc = carry
        st = jnp.dot(k_ref[0, 0, j], qt, preferred_element_type=F32)
        m_new = jnp.maximum(m, jnp.max(st, axis=0, keepdims=True))
        p = jnp.exp2(st - m_new).astype(BF16)
        alpha = jnp.exp2(m - m_new)
        acc = alpha * acc + jnp.dot(vt_ref[0, 0, j], p, preferred_element_type=F32)
        return m_new, acc

    m0 = jnp.full((1, tq), NEG_BIG, F32)
    acc0 = jnp.zeros((SOFTMAX_ROWS, tq), F32)
    _, acc = lax.fori_loop(0, nkb, step, (m0, acc0))
    o_ref[0, 0] = (acc[:ATT_HEAD_DIM] / acc[ATT_HEAD_DIM:ATT_HEAD_DIM + 1]).astype(o_ref.dtype)


def _attn_call(body, name, qt, kb, vtb, tq):
    bsz, g, hd, nq = qt.shape
    nkb, tk = kb.shape[2], kb.shape[3]
    kern = functools.partial(body, nkb=nkb)
    est = 2 * (kb.size // (bsz * g)) * 2 * 2 + 2 * (vtb.size // (bsz * g)) * 2 + 8 * tk * tq * 4
    return pl.pallas_call(
        kern,
        grid=(bsz, g, nq // tq),
        in_specs=[
            pl.BlockSpec((1, 1, hd, tq), lambda b, gg, i: (b, gg, 0, i)),
            pl.BlockSpec((1, 1) + kb.shape[2:], lambda b, gg, i: (b, gg, 0, 0, 0)),
            pl.BlockSpec((1, 1) + vtb.shape[2:], lambda b, gg, i: (b, gg, 0, 0, 0)),
        ],
        out_specs=pl.BlockSpec((1, 1, hd, tq), lambda b, gg, i: (b, gg, 0, i)),
        out_shape=jax.ShapeDtypeStruct((bsz, g, hd, nq), BF16),
        compiler_params=pltpu.CompilerParams(
            dimension_semantics=("arbitrary", "arbitrary", "arbitrary"),
            vmem_limit_bytes=_vmem_limit(est)),
        name=name,
    )(qt, kb, vtb)


def _key_block(nk):
    for tk in (768, 512, 256, 128):
        if nk % tk == 0:
            return tk
    raise ValueError(f"unsupported key count {nk}")


def _attention(q, k_all, v_all, tq, bounded):
    bsz, n, _ = q.shape
    nk = k_all.shape[1]
    tk = _key_block(nk)
    g, r, hd = ATT_KV_HEADS, ATT_REP, ATT_HEAD_DIM
    qt = q.reshape(bsz, n, g, r, hd).transpose(0, 2, 4, 3, 1).reshape(bsz, g, hd, r * n)
    kb = k_all.reshape(bsz, nk, g, hd).transpose(0, 2, 1, 3).reshape(bsz, g, nk // tk, tk, hd)
    vt = v_all.reshape(bsz, nk, g, hd).transpose(0, 2, 3, 1)
    ones = jnp.ones((bsz, g, 1, nk), BF16)
    pad = jnp.zeros((bsz, g, SOFTMAX_ROWS - hd - 1, nk), BF16)
    vtb = jnp.concatenate([vt, ones, pad], axis=2)
    vtb = vtb.reshape(bsz, g, SOFTMAX_ROWS, nk // tk, tk).transpose(0, 1, 3, 2, 4)
    ot = lax.cond(
        bounded,
        functools.partial(_attn_call, _attn_bounded_kernel, "flash_attn_bounded", tq=tq),
        functools.partial(_attn_call, _attn_online_kernel, "flash_attn_online", tq=tq),
        qt, kb, vtb)
    return ot.reshape(bsz, g, hd, r, n).transpose(0, 4, 1, 3, 2).reshape(bsz, n, ATT_WIDTH)


def _outproj_kernel(x_ref, a_ref, att_ref, cm_ref, mod_ref, w_ref, o_ref):
    mix = (jnp.dot(a_ref[0], w_ref[:A_WIDTH, :], preferred_element_type=F32)
           + jnp.dot(att_ref[0], w_ref[A_WIDTH:A_WIDTH + ATT_WIDTH, :], preferred_element_type=F32)
           + jnp.dot(cm_ref[0], w_ref[A_WIDTH + ATT_WIDTH:, :], preferred_element_type=F32))
    o_ref[0] = x_ref[0] + mod_ref[0][2:3, :] * mix


def _outproj_call(x, a, att, cm, mod, w_out, tm):
    bsz, n, d = x.shape
    row = lambda w: pl.BlockSpec((1, tm, w), lambda b, i: (b, i, 0))
    est = 6 * tm * d * 4 + w_out.size * 2 + 4 * tm * d * 2
    return pl.pallas_call(
        _outproj_kernel,
        grid=(bsz, n // tm),
        in_specs=[row(d), row(A_WIDTH), row(ATT_WIDTH), row(C_WIDTH),
                  pl.BlockSpec((1, N_MOD, d), lambda b, i: (b, 0, 0)),
                  _const_spec(w_out.shape)],
        out_specs=row(d),
        out_shape=jax.ShapeDtypeStruct((bsz, n, d), F32),
        compiler_params=pltpu.CompilerParams(
            dimension_semantics=("arbitrary", "arbitrary"),
            vmem_limit_bytes=_vmem_limit(est)),
        name="outproj",
    )(x, a, att, cm, mod, w_out)


def _ffn_kernel(x_ref, xp_ref, xn_ref, mod_ref, g_ref, wu_ref, wc_ref, wd_ref, fg_ref,
                o_ref, acc_ref, *, tm, nchunk, cf, final_norm):
    i = pl.program_id(1)
    last = pl.num_programs(1) - 1
    pm = (i > 0).astype(F32)
    nm = (i < last).astype(F32)

    x = x_ref[0]
    xa = jnp.concatenate([x, xp_ref[0], xn_ref[0]], axis=0)
    mod = mod_ref[0]
    h = _norm_modulate(xa, g_ref[...], mod[3:4, :], mod[4:5, :]).astype(BF16)
    for j in range(nchunk):
        up = jnp.dot(h, wu_ref[:, j * 2 * cf:(j + 1) * 2 * cf], preferred_element_type=F32)
        prev_row = up[tm + HALO - 1:tm + HALO] * pm
        next_row = up[tm + HALO:tm + HALO + 1] * nm
        cv = _conv3_rows(up[:tm], prev_row, next_row, wc_ref[:, j * 2 * cf:(j + 1) * 2 * cf])
        a_ = cv[:, :cf]
        g_ = cv[:, cf:]
        act = ((g_ * _sigmoid(g_)) * a_).astype(BF16)
        part = jnp.dot(act, wd_ref[j * cf:(j + 1) * cf, :], preferred_element_type=F32)
        if j == 0:
            acc_ref[...] = part
        else:
            acc_ref[...] += part
    y = x + mod[5:6, :] * acc_ref[...]
    if final_norm:
        ms = jnp.mean(y * y, axis=-1, keepdims=True)
        y = y * lax.rsqrt(ms + EPS) * fg_ref[...]
    o_ref[0] = y


def _ffn_call(x, mod, g2, wu, wc, wd, fg, tm, final_norm):
    bsz, n, d = x.shape
    d_ff = wd.shape[0]
    cf = 256
    nchunk = d_ff // cf
    kern = functools.partial(_ffn_kernel, tm=tm, nchunk=nchunk, cf=cf, final_norm=final_norm)
    est = (wu.size + wd.size) * 2 + 10 * tm * d * 4 + 8 * tm * 2 * cf * 4
    return pl.pallas_call(
        kern,
        grid=(bsz, n // tm),
        in_specs=_row_tile_specs(tm, n, d) + [
            pl.BlockSpec((1, N_MOD, d), lambda b, i: (b, 0, 0)),
            _const_spec((1, d)),
            _const_spec(wu.shape),
            _const_spec(wc.shape),
            _const_spec(wd.shape),
            _const_spec((1, d)),
        ],
        out_specs=pl.BlockSpec((1, tm, d), lambda b, i: (b, i, 0)),
        out_shape=jax.ShapeDtypeStruct((bsz, n, d), F32),
        scratch_shapes=[pltpu.VMEM((tm, d), F32)],
        compiler_params=pltpu.CompilerParams(
            dimension_semantics=("arbitrary", "arbitrary"),
            vmem_limit_bytes=_vmem_limit(est)),
        name="convffn",
    )(x, x, x, mod, g2, wu, wc, wd, fg)


def _rope_tables(n):
    rows = n // GRID_W
    row = jnp.repeat(jnp.arange(rows), GRID_W).astype(F32)
    col = jnp.tile(jnp.arange(GRID_W), rows).astype(F32)
    inv = ROPE_THETA ** (-2.0 * jnp.arange(ROPE_AXIS_FREQS, dtype=F32) / (2 * ROPE_AXIS_FREQS))
    ar = row[:, None] * inv
    ac = col[:, None] * inv
    cos_h = jnp.concatenate([jnp.cos(ar), jnp.cos(ar), jnp.cos(ac), jnp.cos(ac)], axis=1)
    sin_h = jnp.concatenate([-jnp.sin(ar), jnp.sin(ar), -jnp.sin(ac), jnp.sin(ac)], axis=1)
    return jnp.tile(cos_h, (1, 2)), jnp.tile(sin_h, (1, 2))


def _block_ones(width, block):
    idx = jnp.arange(width) // block
    return (idx[:, None] == idx[None, :]).astype(BF16)


def _interleave_gate(w, d_ff, cf):
    lead = w.shape[:-1]
    a = w[..., :d_ff].reshape(lead + (d_ff // cf, cf))
    g = w[..., d_ff:].reshape(lead + (d_ff // cf, cf))
    return jnp.concatenate([a, g], axis=-1).reshape(lead + (2 * d_ff,))


def kernel(x, c, ctx, c_ctx, w_mod, b_mod, norm1_g, w_in, q_norm_g, k_norm_g, gmlp_w, gmlp_b,
           conv_c_w, w_out, norm2_g, ffn_up, ffn_conv_w, ffn_down, final_g):
    bsz, n, d = x.shape
    n_ctx = ctx.shape[1]
    depth = w_mod.shape[0]
    d_ff = ffn_down.shape[1]
    assert n % 512 == 0 and n_ctx % CHUNK == 0 and bsz + 1 <= V7X_SUBLANES

    tm_lat, tm_ctx = 512, n_ctx
    tq = 512

    c_rows = jnp.concatenate(
        [c, c_ctx[None, :], jnp.zeros((V7X_SUBLANES - bsz - 1, d), F32)], axis=0)
    mod_all = _mod_call(c_rows, w_mod, b_mod)

    cos_l, sin_l = _rope_tables(n)
    cos_c = jnp.ones((n_ctx, V7X_LANES), F32)
    sin_c = jnp.zeros((n_ctx, V7X_LANES), F32)
    jmat = _block_ones(ATT_WIDTH, ATT_HEAD_DIM)
    final_row = final_g.reshape(1, d)

    xc = ctx
    for l in range(depth):
        last = l == depth - 1
        mod_lat = mod_all[l, :bsz].reshape(bsz, N_MOD, d)
        mod_ctx = jnp.broadcast_to(mod_all[l, bsz].reshape(1, N_MOD, d), (bsz, N_MOD, d))
        w_in_b = w_in[l].astype(BF16)
        w_out_b = w_out[l].astype(BF16)
        wu = _interleave_gate(ffn_up[l], d_ff, 256).astype(BF16)
        wc = _interleave_gate(ffn_conv_w[l], d_ff, 256)
        wd = ffn_down[l].astype(BF16)
        g1 = norm1_g[l].reshape(1, d)
        g2 = norm2_g[l].reshape(1, d)
        qg = jnp.tile(q_norm_g[l], ATT_Q_HEADS).reshape(1, ATT_WIDTH)
        kg = jnp.tile(k_norm_g[l], ATT_KV_HEADS).reshape(1, ATT_KV_WIDTH)
        gw = gmlp_w[l].transpose(1, 0, 2).reshape(CHUNK, A_HEADS * CHUNK).astype(BF16)
        gb = jnp.repeat(gmlp_b[l].T, A_HEAD_DIM, axis=1)
        cw = conv_c_w[l]
        shared = (w_in_b, qg, kg)
        score_bound = (1.02 * ATT_HEAD_DIM * ATT_SCALE * LOG2_E
                       * jnp.max(jnp.abs(q_norm_g[l])) * jnp.max(jnp.abs(k_norm_g[l])))
        bounded = score_bound <= SCORE_BOUND_LOG2
        tail = (jmat, gw, gb, cw)

        q_c, k_c, v_c, a_c, cm_c = _inproj_call(
            xc, mod_ctx, g1, *shared, cos_c, sin_c, *tail, tm=tm_ctx)
        if not last:
            att_c = _attention(q_c, k_c, v_c, tq, bounded)
            xc = _outproj_call(xc, a_c, att_c, cm_c, mod_ctx, w_out_b, tm_ctx)
            xc = _ffn_call(xc, mod_ctx, g2, wu, wc, wd, final_row, tm_ctx, False)

        q_l, k_l, v_l, a_l, cm_l = _inproj_call(
            x, mod_lat, g1, *shared, cos_l, sin_l, *tail, tm=tm_lat)
        k_all = jnp.concatenate([k_c, k_l], axis=1)
        v_all = jnp.concatenate([v_c, v_l], axis=1)
        att = _attention(q_l, k_all, v_all, tq, bounded)
        x = _outproj_call(x, a_l, att, cm_l, mod_lat, w_out_b, 1024)
        x = _ffn_call(x, mod_lat, g2, wu, wc, wd, final_row, tm_lat, last)
    return x
```

```python
import functools

import jax
import jax.numpy as jnp
from jax import lax
from jax.experimental import pallas as pl
from jax.experimental.pallas import tpu as pltpu

F32 = jnp.float32
BF16 = jnp.bfloat16

EPS = 1e-6
ROPE_THETA = 10000.0
GRID_W = 64
CHUNK = 128
N_MOD = 6

A_HEADS = 4
A_HEAD_DIM = 64
A_WIDTH = A_HEADS * A_HEAD_DIM
ATT_Q_HEADS = 8
ATT_KV_HEADS = 2
ATT_HEAD_DIM = 64
ATT_REP = ATT_Q_HEADS // ATT_KV_HEADS
ATT_WIDTH = ATT_Q_HEADS * ATT_HEAD_DIM
ATT_KV_WIDTH = ATT_KV_HEADS * ATT_HEAD_DIM
ATT_SCALE = ATT_HEAD_DIM ** -0.5
LOG2_E = 1.4426950408889634
ROPE_AXIS_FREQS = ATT_HEAD_DIM // 4
C_WIDTH = 256

OFF_AU = 0
OFF_AV = OFF_AU + A_WIDTH
OFF_Q = OFF_AV + A_WIDTH
OFF_K = OFF_Q + ATT_WIDTH
OFF_V = OFF_K + ATT_KV_WIDTH
OFF_CB = OFF_V + ATT_KV_WIDTH
OFF_CC = OFF_CB + C_WIDTH
OFF_CH = OFF_CC + C_WIDTH
D_IN = OFF_CH + C_WIDTH

V7X_SUBLANES = 8
V7X_LANES = 128
V7X_VMEM_BYTES = 64 * 1024 * 1024
HALO = V7X_SUBLANES
BF16_ROWS = 16
Q_POS = V7X_LANES
SOFTMAX_ROWS = ATT_HEAD_DIM + BF16_ROWS
NEG_BIG = -1e30
SCORE_BOUND_LOG2 = 50.0


def _vmem_limit(nbytes):
    return int(min(max(nbytes, 16 * 1024 * 1024), V7X_VMEM_BYTES - 6 * 1024 * 1024))


def _sigmoid(x):
    return 1.0 / (1.0 + jnp.exp(-x))


def _gelu_tanh(x):
    c = 0.7978845608028654
    return x * (0.5 * (1.0 + jnp.tanh(c * (x + 0.044715 * (x * x * x)))))


def _split_dot(x, j):
    hi = x.astype(BF16)
    lo = (x - hi.astype(F32)).astype(BF16)
    return (jnp.dot(hi, j, preferred_element_type=F32)
            + jnp.dot(lo, j, preferred_element_type=F32))


def _norm_modulate(x, g, shift, scale):
    ms = jnp.mean(x * x, axis=-1, keepdims=True)
    return (x * lax.rsqrt(ms + EPS) * g) * (1.0 + scale) + shift


def _conv3_rows(u, prev_row, next_row, w3):
    rows = u.shape[0]
    rid = lax.broadcasted_iota(jnp.int32, u.shape, 0)
    dn = jnp.where(rid == 0, prev_row, pltpu.roll(u, 1, 0))
    up = jnp.where(rid == rows - 1, next_row, pltpu.roll(u, rows - 1, 0))
    return dn * w3[0:1, :] + u * w3[1:2, :] + up * w3[2:3, :]


def _rope128(x, cos_t, sin_t, odd16):
    sw = jnp.where(odd16, pltpu.roll(x, 16, 1), pltpu.roll(x, V7X_LANES - 16, 1))
    return x * cos_t + sw * sin_t


def _mod_kernel(c_ref, w_ref, b_ref, o_ref):
    c = c_ref[...]
    s = (c * _sigmoid(c)).astype(BF16)
    o_ref[0] = jnp.dot(s, w_ref[0].astype(BF16), preferred_element_type=F32) + b_ref[0]


def _mod_call(c_rows, w_mod, b_mod):
    depth, d, nm = w_mod.shape
    bn = 1536
    return pl.pallas_call(
        _mod_kernel,
        grid=(depth, nm // bn),
        in_specs=[
            pl.BlockSpec((V7X_SUBLANES, d), lambda l, j: (0, 0)),
            pl.BlockSpec((1, d, bn), lambda l, j: (l, 0, j)),
            pl.BlockSpec((1, 1, bn), lambda l, j: (l, 0, j)),
        ],
        out_specs=pl.BlockSpec((1, V7X_SUBLANES, bn), lambda l, j: (l, 0, j)),
        out_shape=jax.ShapeDtypeStruct((depth, V7X_SUBLANES, nm), F32),
        compiler_params=pltpu.CompilerParams(
            dimension_semantics=("arbitrary", "arbitrary"),
            vmem_limit_bytes=_vmem_limit(3 * d * bn * 4)),
        name="adaln_mod",
    )(c_rows, w_mod, b_mod.reshape(depth, 1, nm))


def _inproj_kernel(x_ref, xp_ref, xn_ref, mod_ref, g_ref, w_ref, qg_ref, kg_ref,
                   cos_ref, sin_ref, j_ref, gw_ref, gb_ref, cw_ref,
                   q_ref, k_ref, v_ref, a_ref, cm_ref, *, tm):
    i = pl.program_id(1)
    last = pl.num_programs(1) - 1
    pm = (i > 0).astype(F32)
    nm = (i < last).astype(F32)

    xa = jnp.concatenate([x_ref[0], xp_ref[0], xn_ref[0]], axis=0)
    mod = mod_ref[0]
    h = _norm_modulate(xa, g_ref[...], mod[0:1, :], mod[1:2, :]).astype(BF16)
    proj = jnp.dot(h[:tm], w_ref[:, :OFF_CC], preferred_element_type=F32)
    pch = jnp.dot(h, w_ref[:, OFF_CC:], preferred_element_type=F32)

    cos_t = cos_ref[...]
    sin_t = sin_ref[...]
    lane = lax.broadcasted_iota(jnp.int32, (tm, V7X_LANES), 1)
    odd16 = (lane & 16) != 0
    inv_hd = 1.0 / ATT_HEAD_DIM

    q = proj[:, OFF_Q:OFF_K]
    qn = q * lax.rsqrt(_split_dot(q * q, j_ref[...]) * inv_hd + EPS) * qg_ref[...]
    heads_per_blk = V7X_LANES // ATT_HEAD_DIM
    for t in range(ATT_WIDTH // V7X_LANES):
        blk = _rope128(qn[:, t * V7X_LANES:(t + 1) * V7X_LANES], cos_t, sin_t, odd16)
        blk_t = (blk * (ATT_SCALE * LOG2_E)).T.astype(BF16)
        for u in range(heads_per_blk):
            gi, ri = divmod(t * heads_per_blk + u, ATT_REP)
            for s in range(tm // Q_POS):
                col = s * ATT_REP * Q_POS + ri * Q_POS
                q_ref[0, gi, :, col:col + Q_POS] = blk_t[u * ATT_HEAD_DIM:(u + 1) * ATT_HEAD_DIM,
                                                         s * Q_POS:(s + 1) * Q_POS]
    k = proj[:, OFF_K:OFF_V]
    kn = k * lax.rsqrt(_split_dot(k * k, j_ref[:ATT_KV_WIDTH, :ATT_KV_WIDTH]) * inv_hd + EPS) * kg_ref[...]
    k_ref[0] = _rope128(kn, cos_t, sin_t, odd16).astype(BF16)
    v_ref[0] = proj[:, OFF_V:OFF_CB].T.astype(BF16)

    u_act = _gelu_tanh(proj[:, OFF_AU:OFF_AV])
    v_act = _gelu_tanh(proj[:, OFF_AV:OFF_Q])
    jl = j_ref[:A_WIDTH, :A_WIDTH]
    mu = _split_dot(v_act, jl) * (1.0 / A_HEAD_DIM)
    xc = v_act - mu
    var = _split_dot(xc * xc, jl) * (1.0 / A_HEAD_DIM)
    vln = xc * lax.rsqrt(var + EPS)
    lane_a = lax.broadcasted_iota(jnp.int32, (CHUNK, A_WIDTH), 1)
    for c in range(tm // CHUNK):
        vc = vln[c * CHUNK:(c + 1) * CHUNK]
        vs = jnp.concatenate(
            [jnp.where((lane_a // A_HEAD_DIM) == hh, vc, 0.0).astype(BF16) for hh in range(A_HEADS)],
            axis=0)
        mixed = jnp.dot(gw_ref[...], vs, preferred_element_type=F32) + gb_ref[...]
        a_ref[0, c * CHUNK:(c + 1) * CHUNK, :] = (u_act[c * CHUNK:(c + 1) * CHUNK] * mixed).astype(BF16)

    ch = pch[:, :C_WIDTH] * pch[:, C_WIDTH:]
    prev_row = ch[tm + HALO - 1:tm + HALO] * pm
    next_row = ch[tm + HALO:tm + HALO + 1] * nm
    cm = proj[:, OFF_CB:OFF_CC] * _conv3_rows(ch[:tm], prev_row, next_row, cw_ref[...])
    cm_ref[0] = cm.astype(BF16)


def _row_tile_specs(tm, n, d):
    nb8 = n // HALO
    r = tm // HALO
    return [
        pl.BlockSpec((1, tm, d), lambda b, i: (b, i, 0)),
        pl.BlockSpec((1, HALO, d), lambda b, i: (b, jnp.maximum(i * r - 1, 0), 0)),
        pl.BlockSpec((1, HALO, d), lambda b, i: (b, jnp.minimum((i + 1) * r, nb8 - 1), 0)),
    ]


def _const_spec(shape):
    nd = len(shape)
    return pl.BlockSpec(shape, lambda b, i: (0,) * nd, pipeline_mode=pl.Buffered(1))


def _inproj_call(x, mod, g1, w_in, qg, kg, cos_t, sin_t, jmat, gw, gb, cw, tm):
    bsz, n, d = x.shape
    kern = functools.partial(_inproj_kernel, tm=tm)
    row = lambda w: pl.BlockSpec((1, tm, w), lambda b, i: (b, i, 0))
    est = (4 * tm * d * 4 + w_in.size * 2 + 8 * tm * D_IN * 4 + 4 * tm * 2048 * 2)
    return pl.pallas_call(
        kern,
        grid=(bsz, n // tm),
        in_specs=_row_tile_specs(tm, n, d) + [
            pl.BlockSpec((1, N_MOD, d), lambda b, i: (b, 0, 0)),
            _const_spec((1, d)),
            _const_spec(w_in.shape),
            _const_spec(qg.shape),
            _const_spec(kg.shape),
            pl.BlockSpec((tm, V7X_LANES), lambda b, i: (i, 0)),
            pl.BlockSpec((tm, V7X_LANES), lambda b, i: (i, 0)),
            _const_spec(jmat.shape),
            _const_spec(gw.shape),
            _const_spec(gb.shape),
            _const_spec(cw.shape),
        ],
        out_specs=[
            pl.BlockSpec((1, ATT_KV_HEADS, ATT_HEAD_DIM, ATT_REP * tm), lambda b, i: (b, 0, 0, i)),
            row(ATT_KV_WIDTH),
            pl.BlockSpec((1, ATT_KV_WIDTH, tm), lambda b, i: (b, 0, i)),
            row(A_WIDTH), row(C_WIDTH)],
        out_shape=[
            jax.ShapeDtypeStruct((bsz, ATT_KV_HEADS, ATT_HEAD_DIM, ATT_REP * n), BF16),
            jax.ShapeDtypeStruct((bsz, n, ATT_KV_WIDTH), BF16),
            jax.ShapeDtypeStruct((bsz, ATT_KV_WIDTH, n), BF16),
            jax.ShapeDtypeStruct((bsz, n, A_WIDTH), BF16),
            jax.ShapeDtypeStruct((bsz, n, C_WIDTH), BF16),
        ],
        compiler_params=pltpu.CompilerParams(
            dimension_semantics=("arbitrary", "arbitrary"),
            vmem_limit_bytes=_vmem_limit(est)),
        name="inproj",
    )(x, x, x, mod, g1, w_in, qg, kg, cos_t, sin_t, jmat, gw, gb, cw)


def _attn_operands(qt_ref, k_ref, vt_ref, tk):
    g = pl.program_id(1)
    qt = qt_ref[0, 0]
    zero = jnp.zeros_like(qt)
    qpad = jnp.concatenate([jnp.where(g == gg, qt, zero) for gg in range(ATT_KV_HEADS)], axis=0)
    ones_rows = (lax.broadcasted_iota(jnp.int32, (BF16_ROWS, tk), 0) == 0).astype(BF16)

    def scores(j):
        return jnp.dot(k_ref[0, j * tk:(j + 1) * tk, :], qpad, preferred_element_type=F32)

    def values(j):
        return jnp.concatenate([vt_ref[0, :, j * tk:(j + 1) * tk], ones_rows], axis=0)

    return scores, values


def _attn_finish(acc, o_ref):
    o = acc[:ATT_HEAD_DIM] / acc[ATT_HEAD_DIM:ATT_HEAD_DIM + 1]
    heads_per_blk = V7X_LANES // ATT_HEAD_DIM
    for p in range(ATT_REP // heads_per_blk):
        blk = jnp.concatenate(
            [o[:, (p * heads_per_blk + u) * Q_POS:(p * heads_per_blk + u + 1) * Q_POS]
             for u in range(heads_per_blk)], axis=0)
        o_ref[0, :, p * V7X_LANES:(p + 1) * V7X_LANES] = blk.T.astype(o_ref.dtype)


def _attn_bounded_kernel(qt_ref, k_ref, vt_ref, o_ref, *, tk):
    scores, values = _attn_operands(qt_ref, k_ref, vt_ref, tk)
    nkb = k_ref.shape[1] // tk
    acc = None
    st_next = scores(0)
    for j in range(nkb):
        st = st_next
        if j + 1 < nkb:
            st_next = scores(j + 1)
        part = jnp.dot(values(j), jnp.exp2(st).astype(BF16), preferred_element_type=F32)
        acc = part if acc is None else acc + part
    _attn_finish(acc, o_ref)


def _attn_online_kernel(qt_ref, k_ref, vt_ref, o_ref, *, tk):
    scores, values = _attn_operands(qt_ref, k_ref, vt_ref, tk)
    nkb = k_ref.shape[1] // tk
    tq = qt_ref.shape[3]
    m = jnp.full((1, tq), NEG_BIG, F32)
    acc = jnp.zeros((SOFTMAX_ROWS, tq), F32)
    for j in range(nkb):
        st = scores(j)
        m_new = jnp.maximum(m, jnp.max(st, axis=0, keepdims=True))
        p = jnp.exp2(st - m_new).astype(BF16)
        acc = jnp.exp2(m - m_new) * acc + jnp.dot(values(j), p, preferred_element_type=F32)
        m = m_new
    _attn_finish(acc, o_ref)


def _attn_call(body, name, qt, k_all, vt_all):
    bsz, g, hd, nq = qt.shape
    nk = k_all.shape[1]
    n = nq // ATT_REP
    tq = ATT_REP * Q_POS
    tk = _key_block(nk)
    kern = functools.partial(body, tk=tk)
    est = 4 * nk * V7X_LANES * 2 + 4 * hd * nk * 2 + 10 * tk * tq * 4
    return pl.pallas_call(
        kern,
        grid=(bsz, g, nq // tq),
        in_specs=[
            pl.BlockSpec((1, 1, hd, tq), lambda b, gg, i: (b, gg, 0, i)),
            pl.BlockSpec((1, nk, ATT_KV_WIDTH), lambda b, gg, i: (b, 0, 0)),
            pl.BlockSpec((1, hd, nk), lambda b, gg, i: (b, gg, 0)),
        ],
        out_specs=pl.BlockSpec((1, Q_POS, ATT_REP * hd), lambda b, gg, i: (b, i, gg)),
        out_shape=jax.ShapeDtypeStruct((bsz, n, ATT_WIDTH), BF16),
        compiler_params=pltpu.CompilerParams(
            dimension_semantics=("arbitrary", "arbitrary", "arbitrary"),
            vmem_limit_bytes=_vmem_limit(est)),
        name=name,
    )(qt, k_all, vt_all)


def _key_block(nk):
    for tk in (768, 512, 256, 128):
        if nk % tk == 0:
            return tk
    raise ValueError(f"unsupported key count {nk}")


def _attention(qt, k_all, vt_all, bounded):
    return lax.cond(
        bounded,
        functools.partial(_attn_call, _attn_bounded_kernel, "flash_attn_bounded"),
        functools.partial(_attn_call, _attn_online_kernel, "flash_attn_online"),
        qt, k_all, vt_all)


def _outproj_kernel(x_ref, a_ref, att_ref, cm_ref, mod_ref, w_ref, o_ref):
    mix = (jnp.dot(a_ref[0], w_ref[:A_WIDTH, :], preferred_element_type=F32)
           + jnp.dot(att_ref[0], w_ref[A_WIDTH:A_WIDTH + ATT_WIDTH, :], preferred_element_type=F32)
           + jnp.dot(cm_ref[0], w_ref[A_WIDTH + ATT_WIDTH:, :], preferred_element_type=F32))
    o_ref[0] = x_ref[0] + mod_ref[0][2:3, :] * mix


def _outproj_call(x, a, att, cm, mod, w_out, tm):
    bsz, n, d = x.shape
    row = lambda w: pl.BlockSpec((1, tm, w), lambda b, i: (b, i, 0))
    est = 6 * tm * d * 4 + w_out.size * 2 + 4 * tm * d * 2
    return pl.pallas_call(
        _outproj_kernel,
        grid=(bsz, n // tm),
        in_specs=[row(d), row(A_WIDTH), row(ATT_WIDTH), row(C_WIDTH),
                  pl.BlockSpec((1, N_MOD, d), lambda b, i: (b, 0, 0)),
                  _const_spec(w_out.shape)],
        out_specs=row(d),
        out_shape=jax.ShapeDtypeStruct((bsz, n, d), F32),
        compiler_params=pltpu.CompilerParams(
            dimension_semantics=("arbitrary", "arbitrary"),
            vmem_limit_bytes=_vmem_limit(est)),
        name="outproj",
    )(x, a, att, cm, mod, w_out)


def _ffn_kernel(x_ref, xp_ref, xn_ref, mod_ref, g_ref, wu_ref, wc_ref, wd_ref, fg_ref,
                o_ref, acc_ref, *, tm, nchunk, cf, final_norm):
    i = pl.program_id(1)
    last = pl.num_programs(1) - 1
    pm = (i > 0).astype(F32)
    nm = (i < last).astype(F32)

    x = x_ref[0]
    xa = jnp.concatenate([x, xp_ref[0], xn_ref[0]], axis=0)
    mod = mod_ref[0]
    h = _norm_modulate(xa, g_ref[...], mod[3:4, :], mod[4:5, :]).astype(BF16)
    def up_proj(j):
        return jnp.dot(h, wu_ref[:, j * 2 * cf:(j + 1) * 2 * cf], preferred_element_type=F32)

    up_next = up_proj(0)
    for j in range(nchunk):
        up = up_next
        if j + 1 < nchunk:
            up_next = up_proj(j + 1)
        prev_row = up[tm + HALO - 1:tm + HALO] * pm
        next_row = up[tm + HALO:tm + HALO + 1] * nm
        cv = _conv3_rows(up[:tm], prev_row, next_row, wc_ref[:, j * 2 * cf:(j + 1) * 2 * cf])
        a_ = cv[:, :cf]
        g_ = cv[:, cf:]
        act = ((g_ * _sigmoid(g_)) * a_).astype(BF16)
        part = jnp.dot(act, wd_ref[j * cf:(j + 1) * cf, :], preferred_element_type=F32)
        if j == 0:
            acc_ref[...] = part
        else:
            acc_ref[...] += part
    y = x + mod[5:6, :] * acc_ref[...]
    if final_norm:
        ms = jnp.mean(y * y, axis=-1, keepdims=True)
        y = y * lax.rsqrt(ms + EPS) * fg_ref[...]
    o_ref[0] = y


def _ffn_call(x, mod, g2, wu, wc, wd, fg, tm, final_norm):
    bsz, n, d = x.shape
    d_ff = wd.shape[0]
    cf = 256
    nchunk = d_ff // cf
    kern = functools.partial(_ffn_kernel, tm=tm, nchunk=nchunk, cf=cf, final_norm=final_norm)
    est = (wu.size + wd.size) * 2 + 10 * tm * d * 4 + 8 * tm * 2 * cf * 4
    return pl.pallas_call(
        kern,
        grid=(bsz, n // tm),
        in_specs=_row_tile_specs(tm, n, d) + [
            pl.BlockSpec((1, N_MOD, d), lambda b, i: (b, 0, 0)),
            _const_spec((1, d)),
            _const_spec(wu.shape),
            _const_spec(wc.shape),
            _const_spec(wd.shape),
            _const_spec((1, d)),
        ],
        out_specs=pl.BlockSpec((1, tm, d), lambda b, i: (b, i, 0)),
        out_shape=jax.ShapeDtypeStruct((bsz, n, d), F32),
        scratch_shapes=[pltpu.VMEM((tm, d), F32)],
        compiler_params=pltpu.CompilerParams(
            dimension_semantics=("arbitrary", "arbitrary"),
            vmem_limit_bytes=_vmem_limit(est)),
        name="convffn",
    )(x, x, x, mod, g2, wu, wc, wd, fg)


def _rope_tables(n):
    rows = n // GRID_W
    row = jnp.repeat(jnp.arange(rows), GRID_W).astype(F32)
    col = jnp.tile(jnp.arange(GRID_W), rows).astype(F32)
    inv = ROPE_THETA ** (-2.0 * jnp.arange(ROPE_AXIS_FREQS, dtype=F32) / (2 * ROPE_AXIS_FREQS))
    ar = row[:, None] * inv
    ac = col[:, None] * inv
    cos_h = jnp.concatenate([jnp.cos(ar), jnp.cos(ar), jnp.cos(ac), jnp.cos(ac)], axis=1)
    sin_h = jnp.concatenate([-jnp.sin(ar), jnp.sin(ar), -jnp.sin(ac), jnp.sin(ac)], axis=1)
    return jnp.tile(cos_h, (1, 2)), jnp.tile(sin_h, (1, 2))


def _block_ones(width, block):
    idx = jnp.arange(width) // block
    return (idx[:, None] == idx[None, :]).astype(BF16)


def _interleave_gate(w, d_ff, cf):
    lead = w.shape[:-1]
    a = w[..., :d_ff].reshape(lead + (d_ff // cf, cf))
    g = w[..., d_ff:].reshape(lead + (d_ff // cf, cf))
    return jnp.concatenate([a, g], axis=-1).reshape(lead + (2 * d_ff,))


def kernel(x, c, ctx, c_ctx, w_mod, b_mod, norm1_g, w_in, q_norm_g, k_norm_g, gmlp_w, gmlp_b,
           conv_c_w, w_out, norm2_g, ffn_up, ffn_conv_w, ffn_down, final_g):
    bsz, n, d = x.shape
    n_ctx = ctx.shape[1]
    depth = w_mod.shape[0]
    d_ff = ffn_down.shape[1]
    assert n % 512 == 0 and n_ctx % CHUNK == 0 and bsz + 1 <= V7X_SUBLANES

    tm_lat, tm_ctx = 512, n_ctx

    c_rows = jnp.concatenate(
        [c, c_ctx[None, :], jnp.zeros((V7X_SUBLANES - bsz - 1, d), F32)], axis=0)
    mod_all = _mod_call(c_rows, w_mod, b_mod)

    cos_l, sin_l = _rope_tables(n)
    cos_c = jnp.ones((n_ctx, V7X_LANES), F32)
    sin_c = jnp.zeros((n_ctx, V7X_LANES), F32)
    jmat = _block_ones(ATT_WIDTH, ATT_HEAD_DIM)
    final_row = final_g.reshape(1, d)

    xc = ctx
    for l in range(depth):
        last = l == depth - 1
        mod_lat = mod_all[l, :bsz].reshape(bsz, N_MOD, d)
        mod_ctx = jnp.broadcast_to(mod_all[l, bsz].reshape(1, N_MOD, d), (bsz, N_MOD, d))
        w_in_b = w_in[l].astype(BF16)
        w_out_b = w_out[l].astype(BF16)
        wu = _interleave_gate(ffn_up[l], d_ff, 256).astype(BF16)
        wc = _interleave_gate(ffn_conv_w[l], d_ff, 256)
        wd = ffn_down[l].astype(BF16)
        g1 = norm1_g[l].reshape(1, d)
        g2 = norm2_g[l].reshape(1, d)
        qg = jnp.tile(q_norm_g[l], ATT_Q_HEADS).reshape(1, ATT_WIDTH)
        kg = jnp.tile(k_norm_g[l], ATT_KV_HEADS).reshape(1, ATT_KV_WIDTH)
        gw = gmlp_w[l].transpose(1, 0, 2).reshape(CHUNK, A_HEADS * CHUNK).astype(BF16)
        gb = jnp.repeat(gmlp_b[l].T, A_HEAD_DIM, axis=1)
        cw = conv_c_w[l]
        shared = (w_in_b, qg, kg)
        score_bound = (1.02 * ATT_HEAD_DIM * ATT_SCALE * LOG2_E
                       * jnp.max(jnp.abs(q_norm_g[l])) * jnp.max(jnp.abs(k_norm_g[l])))
        bounded = score_bound <= SCORE_BOUND_LOG2
        tail = (jmat, gw, gb, cw)

        q_c, k_c, v_c, a_c, cm_c = _inproj_call(
            xc, mod_ctx, g1, *shared, cos_c, sin_c, *tail, tm=tm_ctx)
        if not last:
            att_c = _attention(q_c, k_c, v_c, bounded)
            xc = _outproj_call(xc, a_c, att_c, cm_c, mod_ctx, w_out_b, tm_ctx)
            xc = _ffn_call(xc, mod_ctx, g2, wu, wc, wd, final_row, tm_ctx, False)

        q_l, k_l, v_l, a_l, cm_l = _inproj_call(
            x, mod_lat, g1, *shared, cos_l, sin_l, *tail, tm=tm_lat)
        k_all = jnp.concatenate([k_c, k_l], axis=1)
        v_all = jnp.concatenate([v_c, v_l], axis=2)
        att = _attention(q_l, k_all, v_all, bounded)
        x = _outproj_call(x, a_l, att, cm_l, mod_lat, w_out_b, 1024)
        x = _ffn_call(x, mod_lat, g2, wu, wc, wd, final_row, tm_lat, last)
    return x
```

```python
import functools

import jax
import jax.numpy as jnp
from jax import lax
from jax.experimental import pallas as pl
from jax.experimental.pallas import tpu as pltpu

F32 = jnp.float32
BF16 = jnp.bfloat16

EPS = 1e-6
ROPE_THETA = 10000.0
GRID_W = 64
CHUNK = 128
N_MOD = 6

A_HEADS = 4
A_HEAD_DIM = 64
A_WIDTH = A_HEADS * A_HEAD_DIM
ATT_Q_HEADS = 8
ATT_KV_HEADS = 2
ATT_HEAD_DIM = 64
ATT_REP = ATT_Q_HEADS // ATT_KV_HEADS
ATT_WIDTH = ATT_Q_HEADS * ATT_HEAD_DIM
ATT_KV_WIDTH = ATT_KV_HEADS * ATT_HEAD_DIM
ATT_SCALE = ATT_HEAD_DIM ** -0.5
LOG2_E = 1.4426950408889634
ROPE_AXIS_FREQS = ATT_HEAD_DIM // 4
C_WIDTH = 256

OFF_AU = 0
OFF_AV = OFF_AU + A_WIDTH
OFF_Q = OFF_AV + A_WIDTH
OFF_K = OFF_Q + ATT_WIDTH
OFF_V = OFF_K + ATT_KV_WIDTH
OFF_CB = OFF_V + ATT_KV_WIDTH
OFF_CC = OFF_CB + C_WIDTH
OFF_CH = OFF_CC + C_WIDTH
D_IN = OFF_CH + C_WIDTH

V7X_SUBLANES = 8
V7X_LANES = 128
V7X_VMEM_BYTES = 64 * 1024 * 1024
HALO = V7X_SUBLANES
BF16_ROWS = 16
Q_POS = V7X_LANES
SOFTMAX_ROWS = ATT_HEAD_DIM + BF16_ROWS
NEG_BIG = -1e30
ATTN_LOOKAHEAD = 1
SCORE_BOUND_LOG2 = 50.0


def _vmem_limit(nbytes):
    return int(min(max(nbytes, 16 * 1024 * 1024), V7X_VMEM_BYTES - 6 * 1024 * 1024))


def _sigmoid(x):
    return 1.0 / (1.0 + jnp.exp(-x))


def _gelu_tanh(x):
    c = 0.7978845608028654
    return x * (0.5 * (1.0 + jnp.tanh(c * (x + 0.044715 * (x * x * x)))))


def _split_dot(x, j):
    hi = x.astype(BF16)
    lo = (x - hi.astype(F32)).astype(BF16)
    return (jnp.dot(hi, j, preferred_element_type=F32)
            + jnp.dot(lo, j, preferred_element_type=F32))


def _norm_modulate(x, g, shift, scale):
    ms = jnp.mean(x * x, axis=-1, keepdims=True)
    return (x * lax.rsqrt(ms + EPS) * g) * (1.0 + scale) + shift


def _conv3_seq(u_all, w3, tm):
    rows = u_all.shape[0]
    dn = pltpu.roll(u_all, 1, 0)[HALO:HALO + tm]
    up = pltpu.roll(u_all, rows - 1, 0)[HALO:HALO + tm]
    return dn * w3[0:1, :] + u_all[HALO:HALO + tm] * w3[1:2, :] + up * w3[2:3, :]


def _halo_norm_modulate(x_ref, xp_ref, xn_ref, g, shift, scale):
    i = pl.program_id(1)
    has_prev = (i > 0).astype(F32)
    has_next = (i < pl.num_programs(1) - 1).astype(F32)
    hm = _norm_modulate(x_ref[0], g, shift, scale)
    hp = _norm_modulate(xp_ref[0], g, shift, scale) * has_prev
    hn = _norm_modulate(xn_ref[0], g, shift, scale) * has_next
    return hm.astype(BF16), jnp.concatenate([hp, hm, hn], axis=0).astype(BF16)


def _rope128(x, cos_t, sin_t, odd16):
    sw = jnp.where(odd16, pltpu.roll(x, 16, 1), pltpu.roll(x, V7X_LANES - 16, 1))
    return x * cos_t + sw * sin_t


def _mod_kernel(c_ref, w_ref, b_ref, o_ref):
    c = c_ref[...]
    s = (c * _sigmoid(c)).astype(BF16)
    o_ref[0] = jnp.dot(s, w_ref[0].astype(BF16), preferred_element_type=F32) + b_ref[0]


def _mod_call(c_rows, w_mod, b_mod):
    depth, d, nm = w_mod.shape
    bn = 1536
    return pl.pallas_call(
        _mod_kernel,
        grid=(depth, nm // bn),
        in_specs=[
            pl.BlockSpec((V7X_SUBLANES, d), lambda l, j: (0, 0)),
            pl.BlockSpec((1, d, bn), lambda l, j: (l, 0, j)),
            pl.BlockSpec((1, 1, bn), lambda l, j: (l, 0, j)),
        ],
        out_specs=pl.BlockSpec((1, V7X_SUBLANES, bn), lambda l, j: (l, 0, j)),
        out_shape=jax.ShapeDtypeStruct((depth, V7X_SUBLANES, nm), F32),
        compiler_params=pltpu.CompilerParams(
            dimension_semantics=("arbitrary", "arbitrary"),
            vmem_limit_bytes=_vmem_limit(3 * d * bn * 4)),
        name="adaln_mod",
    )(c_rows, w_mod, b_mod.reshape(depth, 1, nm))


def _inproj_kernel(x_ref, xp_ref, xn_ref, mod_ref, g_ref, w_ref, qg_ref, kg_ref,
                   cos_ref, sin_ref, j_ref, gw_ref, gb_ref, cw_ref,
                   q_ref, k_ref, v_ref, a_ref, cm_ref, *, tm):
    mod = mod_ref[0]
    h, h_all = _halo_norm_modulate(x_ref, xp_ref, xn_ref, g_ref[...], mod[0:1, :], mod[1:2, :])
    proj = jnp.dot(h, w_ref[:, :OFF_CC], preferred_element_type=F32)
    pch = jnp.dot(h_all, w_ref[:, OFF_CC:], preferred_element_type=F32)

    cos_t = cos_ref[...]
    sin_t = sin_ref[...]
    lane = lax.broadcasted_iota(jnp.int32, (tm, V7X_LANES), 1)
    odd16 = (lane & 16) != 0
    inv_hd = 1.0 / ATT_HEAD_DIM

    q = proj[:, OFF_Q:OFF_K]
    qn = q * lax.rsqrt(_split_dot(q * q, j_ref[...]) * inv_hd + EPS) * qg_ref[...]
    heads_per_blk = V7X_LANES // ATT_HEAD_DIM
    for t in range(ATT_WIDTH // V7X_LANES):
        blk = _rope128(qn[:, t * V7X_LANES:(t + 1) * V7X_LANES], cos_t, sin_t, odd16)
        blk_t = (blk * (ATT_SCALE * LOG2_E)).T.astype(BF16)
        for u in range(heads_per_blk):
            gi, ri = divmod(t * heads_per_blk + u, ATT_REP)
            for s in range(tm // Q_POS):
                col = s * ATT_REP * Q_POS + ri * Q_POS
                q_ref[0, gi, :, col:col + Q_POS] = blk_t[u * ATT_HEAD_DIM:(u + 1) * ATT_HEAD_DIM,
                                                         s * Q_POS:(s + 1) * Q_POS]
    k = proj[:, OFF_K:OFF_V]
    kn = k * lax.rsqrt(_split_dot(k * k, j_ref[:ATT_KV_WIDTH, :ATT_KV_WIDTH]) * inv_hd + EPS) * kg_ref[...]
    k_ref[0] = _rope128(kn, cos_t, sin_t, odd16).astype(BF16)
    v_ref[0] = proj[:, OFF_V:OFF_CB].T.astype(BF16)

    u_act = _gelu_tanh(proj[:, OFF_AU:OFF_AV])
    v_act = _gelu_tanh(proj[:, OFF_AV:OFF_Q])
    jl = j_ref[:A_WIDTH, :A_WIDTH]
    mu = _split_dot(v_act, jl) * (1.0 / A_HEAD_DIM)
    xc = v_act - mu
    var = _split_dot(xc * xc, jl) * (1.0 / A_HEAD_DIM)
    vln = xc * lax.rsqrt(var + EPS)
    lane_a = lax.broadcasted_iota(jnp.int32, (CHUNK, A_WIDTH), 1)
    for c in range(tm // CHUNK):
        vc = vln[c * CHUNK:(c + 1) * CHUNK]
        vs = jnp.concatenate(
            [jnp.where((lane_a // A_HEAD_DIM) == hh, vc, 0.0).astype(BF16) for hh in range(A_HEADS)],
            axis=0)
        mixed = jnp.dot(gw_ref[...], vs, preferred_element_type=F32) + gb_ref[...]
        a_ref[0, c * CHUNK:(c + 1) * CHUNK, :] = (u_act[c * CHUNK:(c + 1) * CHUNK] * mixed).astype(BF16)

    ch = pch[:, :C_WIDTH] * pch[:, C_WIDTH:]
    cm = proj[:, OFF_CB:OFF_CC] * _conv3_seq(ch, cw_ref[...], tm)
    cm_ref[0] = cm.astype(BF16)


def _row_tile_specs(tm, n, d):
    nb8 = n // HALO
    r = tm // HALO
    return [
        pl.BlockSpec((1, tm, d), lambda b, i: (b, i, 0)),
        pl.BlockSpec((1, HALO, d), lambda b, i: (b, jnp.maximum(i * r - 1, 0), 0)),
        pl.BlockSpec((1, HALO, d), lambda b, i: (b, jnp.minimum((i + 1) * r, nb8 - 1), 0)),
    ]


def _const_spec(shape):
    nd = len(shape)
    return pl.BlockSpec(shape, lambda b, i: (0,) * nd, pipeline_mode=pl.Buffered(1))


def _inproj_call(x, mod, g1, w_in, qg, kg, cos_t, sin_t, jmat, gw, gb, cw, tm):
    bsz, n, d = x.shape
    kern = functools.partial(_inproj_kernel, tm=tm)
    row = lambda w: pl.BlockSpec((1, tm, w), lambda b, i: (b, i, 0))
    est = (4 * tm * d * 4 + w_in.size * 2 + 8 * tm * D_IN * 4 + 4 * tm * 2048 * 2)
    return pl.pallas_call(
        kern,
        grid=(bsz, n // tm),
        in_specs=_row_tile_specs(tm, n, d) + [
            pl.BlockSpec((1, N_MOD, d), lambda b, i: (b, 0, 0)),
            _const_spec((1, d)),
            _const_spec(w_in.shape),
            _const_spec(qg.shape),
            _const_spec(kg.shape),
            pl.BlockSpec((tm, V7X_LANES), lambda b, i: (i, 0)),
            pl.BlockSpec((tm, V7X_LANES), lambda b, i: (i, 0)),
            _const_spec(jmat.shape),
            _const_spec(gw.shape),
            _const_spec(gb.shape),
            _const_spec(cw.shape),
        ],
        out_specs=[
            pl.BlockSpec((1, ATT_KV_HEADS, ATT_HEAD_DIM, ATT_REP * tm), lambda b, i: (b, 0, 0, i)),
            row(ATT_KV_WIDTH),
            pl.BlockSpec((1, ATT_KV_WIDTH, tm), lambda b, i: (b, 0, i)),
            row(A_WIDTH), row(C_WIDTH)],
        out_shape=[
            jax.ShapeDtypeStruct((bsz, ATT_KV_HEADS, ATT_HEAD_DIM, ATT_REP * n), BF16),
            jax.ShapeDtypeStruct((bsz, n, ATT_KV_WIDTH), BF16),
            jax.ShapeDtypeStruct((bsz, ATT_KV_WIDTH, n), BF16),
            jax.ShapeDtypeStruct((bsz, n, A_WIDTH), BF16),
            jax.ShapeDtypeStruct((bsz, n, C_WIDTH), BF16),
        ],
        compiler_params=pltpu.CompilerParams(
            dimension_semantics=("arbitrary", "arbitrary"),
            vmem_limit_bytes=_vmem_limit(est)),
        name="inproj",
    )(x, x, x, mod, g1, w_in, qg, kg, cos_t, sin_t, jmat, gw, gb, cw)


def _attn_operands(qt_ref, kv_refs, tk):
    g = pl.program_id(1)
    qt = qt_ref[0, 0]
    zero = jnp.zeros_like(qt)
    qpad = jnp.concatenate([jnp.where(g == gg, qt, zero) for gg in range(ATT_KV_HEADS)], axis=0)
    ones_rows = (lax.broadcasted_iota(jnp.int32, (BF16_ROWS, tk), 0) == 0).astype(BF16)
    k_refs, vt_refs = kv_refs[0::2], kv_refs[1::2]
    starts = [0]
    for kr in k_refs:
        starts.append(starts[-1] + kr.shape[1])

    def pieces(j):
        lo, hi = j * tk, (j + 1) * tk
        return [(s, max(lo, starts[s]) - starts[s], min(hi, starts[s + 1]) - starts[s])
                for s in range(len(k_refs)) if max(lo, starts[s]) < min(hi, starts[s + 1])]

    def cat(parts, axis):
        return parts[0] if len(parts) == 1 else jnp.concatenate(parts, axis=axis)

    def scores(j):
        kb = cat([k_refs[s][0, a:b, :] for s, a, b in pieces(j)], 0)
        return jnp.dot(kb, qpad, preferred_element_type=F32)

    def values(j):
        vb = cat([vt_refs[s][0, :, a:b] for s, a, b in pieces(j)], 1)
        return jnp.concatenate([vb, ones_rows], axis=0)

    return scores, values, starts[-1] // tk


def _attn_finish(acc, o_ref):
    o = acc[:ATT_HEAD_DIM] / acc[ATT_HEAD_DIM:ATT_HEAD_DIM + 1]
    heads_per_blk = V7X_LANES // ATT_HEAD_DIM
    for p in range(ATT_REP // heads_per_blk):
        blk = jnp.concatenate(
            [o[:, (p * heads_per_blk + u) * Q_POS:(p * heads_per_blk + u + 1) * Q_POS]
             for u in range(heads_per_blk)], axis=0)
        o_ref[0, :, p * V7X_LANES:(p + 1) * V7X_LANES] = blk.T.astype(o_ref.dtype)


def _attn_bounded_kernel(qt_ref, *refs, tk):
    *kv_refs, o_ref = refs
    scores, values, nkb = _attn_operands(qt_ref, kv_refs, tk)

    def probs(j):
        return jnp.exp2(scores(j)).astype(BF16)

    acc = None
    ahead = [probs(j) for j in range(min(ATTN_LOOKAHEAD, nkb))]
    for j in range(nkb):
        p = ahead.pop(0)
        if j + ATTN_LOOKAHEAD < nkb:
            ahead.append(probs(j + ATTN_LOOKAHEAD))
        part = jnp.dot(values(j), p, preferred_element_type=F32)
        acc = part if acc is None else acc + part
    _attn_finish(acc, o_ref)


def _attn_online_kernel(qt_ref, *refs, tk):
    *kv_refs, o_ref = refs
    scores, values, nkb = _attn_operands(qt_ref, kv_refs, tk)
    tq = qt_ref.shape[3]
    m = jnp.full((1, tq), NEG_BIG, F32)
    acc = jnp.zeros((SOFTMAX_ROWS, tq), F32)
    for j in range(nkb):
        st = scores(j)
        m_new = jnp.maximum(m, jnp.max(st, axis=0, keepdims=True))
        p = jnp.exp2(st - m_new).astype(BF16)
        acc = jnp.exp2(m - m_new) * acc + jnp.dot(values(j), p, preferred_element_type=F32)
        m = m_new
    _attn_finish(acc, o_ref)


def _attn_call(body, name, qt, *kv):
    bsz, g, hd, nq = qt.shape
    nk = sum(k.shape[1] for k in kv[0::2])
    n = nq // ATT_REP
    tq = ATT_REP * Q_POS
    tk = _key_block(nk)
    kern = functools.partial(body, tk=tk)
    est = 4 * nk * V7X_LANES * 2 + 4 * hd * nk * 2 + 10 * tk * tq * 4
    kv_specs = []
    for k, vt in zip(kv[0::2], kv[1::2]):
        kv_specs.append(pl.BlockSpec((1, k.shape[1], ATT_KV_WIDTH), lambda b, gg, i: (b, 0, 0)))
        kv_specs.append(pl.BlockSpec((1, hd, vt.shape[2]), lambda b, gg, i: (b, gg, 0)))
    return pl.pallas_call(
        kern,
        grid=(bsz, g, nq // tq),
        in_specs=[pl.BlockSpec((1, 1, hd, tq), lambda b, gg, i: (b, gg, 0, i))] + kv_specs,
        out_specs=pl.BlockSpec((1, Q_POS, ATT_REP * hd), lambda b, gg, i: (b, i, gg)),
        out_shape=jax.ShapeDtypeStruct((bsz, n, ATT_WIDTH), BF16),
        compiler_params=pltpu.CompilerParams(
            dimension_semantics=("arbitrary", "arbitrary", "arbitrary"),
            vmem_limit_bytes=_vmem_limit(est)),
        name=name,
    )(qt, *kv)


def _key_block(nk):
    for tk in (768, 512, 256, 128):
        if nk % tk == 0:
            return tk
    raise ValueError(f"unsupported key count {nk}")


def _attention(qt, kv, bounded):
    return lax.cond(
        bounded,
        functools.partial(_attn_call, _attn_bounded_kernel, "flash_attn_bounded"),
        functools.partial(_attn_call, _attn_online_kernel, "flash_attn_online"),
        qt, *kv)


def _outproj_kernel(x_ref, a_ref, att_ref, cm_ref, mod_ref, w_ref, o_ref):
    mix = (jnp.dot(a_ref[0], w_ref[:A_WIDTH, :], preferred_element_type=F32)
           + jnp.dot(att_ref[0], w_ref[A_WIDTH:A_WIDTH + ATT_WIDTH, :], preferred_element_type=F32)
           + jnp.dot(cm_ref[0], w_ref[A_WIDTH + ATT_WIDTH:, :], preferred_element_type=F32))
    o_ref[0] = x_ref[0] + mod_ref[0][2:3, :] * mix


def _outproj_call(x, a, att, cm, mod, w_out, tm):
    bsz, n, d = x.shape
    row = lambda w: pl.BlockSpec((1, tm, w), lambda b, i: (b, i, 0))
    est = 6 * tm * d * 4 + w_out.size * 2 + 4 * tm * d * 2
    return pl.pallas_call(
        _outproj_kernel,
        grid=(bsz, n // tm),
        in_specs=[row(d), row(A_WIDTH), row(ATT_WIDTH), row(C_WIDTH),
                  pl.BlockSpec((1, N_MOD, d), lambda b, i: (b, 0, 0)),
                  _const_spec(w_out.shape)],
        out_specs=row(d),
        out_shape=jax.ShapeDtypeStruct((bsz, n, d), F32),
        compiler_params=pltpu.CompilerParams(
            dimension_semantics=("arbitrary", "arbitrary"),
            vmem_limit_bytes=_vmem_limit(est)),
        name="outproj",
    )(x, a, att, cm, mod, w_out)


def _ffn_kernel(x_ref, xp_ref, xn_ref, mod_ref, g_ref, wu_ref, wc_ref, wd_ref, fg_ref,
                o_ref, *, tm, nchunk, cf, final_norm):
    mod = mod_ref[0]
    _, h = _halo_norm_modulate(x_ref, xp_ref, xn_ref, g_ref[...], mod[3:4, :], mod[4:5, :])
    up_all = jnp.dot(h, wu_ref[...], preferred_element_type=F32)
    acts = []
    for j in range(nchunk):
        cv = _conv3_seq(up_all[:, j * 2 * cf:(j + 1) * 2 * cf], wc_ref[:, j * 2 * cf:(j + 1) * 2 * cf], tm)
        a_ = cv[:, :cf]
        g_ = cv[:, cf:]
        acts.append(((g_ * _sigmoid(g_)) * a_).astype(BF16))
    act = jnp.concatenate(acts, axis=1)
    y = x_ref[0] + mod[5:6, :] * jnp.dot(act, wd_ref[...], preferred_element_type=F32)
    if final_norm:
        ms = jnp.mean(y * y, axis=-1, keepdims=True)
        y = y * lax.rsqrt(ms + EPS) * fg_ref[...]
    o_ref[0] = y


def _ffn_call(x, mod, g2, wu, wc, wd, fg, tm, final_norm):
    bsz, n, d = x.shape
    d_ff = wd.shape[0]
    cf = 256
    nchunk = d_ff // cf
    kern = functools.partial(_ffn_kernel, tm=tm, nchunk=nchunk, cf=cf, final_norm=final_norm)
    est = (wu.size + wd.size) * 2 + 10 * tm * d * 4 + (tm + 2 * HALO) * 2 * d_ff * 4 + 2 * tm * d_ff * 2
    return pl.pallas_call(
        kern,
        grid=(bsz, n // tm),
        in_specs=_row_tile_specs(tm, n, d) + [
            pl.BlockSpec((1, N_MOD, d), lambda b, i: (b, 0, 0)),
            _const_spec((1, d)),
            _const_spec(wu.shape),
            _const_spec(wc.shape),
            _const_spec(wd.shape),
            _const_spec((1, d)),
        ],
        out_specs=pl.BlockSpec((1, tm, d), lambda b, i: (b, i, 0)),
        out_shape=jax.ShapeDtypeStruct((bsz, n, d), F32),
        compiler_params=pltpu.CompilerParams(
            dimension_semantics=("arbitrary", "arbitrary"),
            vmem_limit_bytes=_vmem_limit(est)),
        name="convffn",
    )(x, x, x, mod, g2, wu, wc, wd, fg)


def _rope_tables(n):
    rows = n // GRID_W
    row = jnp.repeat(jnp.arange(rows), GRID_W).astype(F32)
    col = jnp.tile(jnp.arange(GRID_W), rows).astype(F32)
    inv = ROPE_THETA ** (-2.0 * jnp.arange(ROPE_AXIS_FREQS, dtype=F32) / (2 * ROPE_AXIS_FREQS))
    ar = row[:, None] * inv
    ac = col[:, None] * inv
    cos_h = jnp.concatenate([jnp.cos(ar), jnp.cos(ar), jnp.cos(ac), jnp.cos(ac)], axis=1)
    sin_h = jnp.concatenate([-jnp.sin(ar), jnp.sin(ar), -jnp.sin(ac), jnp.sin(ac)], axis=1)
    return jnp.tile(cos_h, (1, 2)), jnp.tile(sin_h, (1, 2))


def _block_ones(width, block):
    idx = jnp.arange(width) // block
    return (idx[:, None] == idx[None, :]).astype(BF16)


def _interleave_gate(w, d_ff, cf):
    lead = w.shape[:-1]
    a = w[..., :d_ff].reshape(lead + (d_ff // cf, cf))
    g = w[..., d_ff:].reshape(lead + (d_ff // cf, cf))
    return jnp.concatenate([a, g], axis=-1).reshape(lead + (2 * d_ff,))


def kernel(x, c, ctx, c_ctx, w_mod, b_mod, norm1_g, w_in, q_norm_g, k_norm_g, gmlp_w, gmlp_b,
           conv_c_w, w_out, norm2_g, ffn_up, ffn_conv_w, ffn_down, final_g):
    bsz, n, d = x.shape
    n_ctx = ctx.shape[1]
    depth = w_mod.shape[0]
    d_ff = ffn_down.shape[1]
    assert n % 512 == 0 and n_ctx % CHUNK == 0 and bsz + 1 <= V7X_SUBLANES

    tm_lat, tm_ctx = 512, n_ctx

    c_rows = jnp.concatenate(
        [c, c_ctx[None, :], jnp.zeros((V7X_SUBLANES - bsz - 1, d), F32)], axis=0)
    mod_all = _mod_call(c_rows, w_mod, b_mod)

    cos_l, sin_l = _rope_tables(n)
    cos_c = jnp.ones((n_ctx, V7X_LANES), F32)
    sin_c = jnp.zeros((n_ctx, V7X_LANES), F32)
    jmat = _block_ones(ATT_WIDTH, ATT_HEAD_DIM)
    final_row = final_g.reshape(1, d)

    xc = ctx
    for l in range(depth):
        last = l == depth - 1
        mod_lat = mod_all[l, :bsz].reshape(bsz, N_MOD, d)
        mod_ctx = jnp.broadcast_to(mod_all[l, bsz].reshape(1, N_MOD, d), (bsz, N_MOD, d))
        w_in_b = w_in[l].astype(BF16)
        w_out_b = w_out[l].astype(BF16)
        wu = _interleave_gate(ffn_up[l], d_ff, 256).astype(BF16)
        wc = _interleave_gate(ffn_conv_w[l], d_ff, 256)
        wd = ffn_down[l].astype(BF16)
        g1 = norm1_g[l].reshape(1, d)
        g2 = norm2_g[l].reshape(1, d)
        qg = jnp.tile(q_norm_g[l], ATT_Q_HEADS).reshape(1, ATT_WIDTH)
        kg = jnp.tile(k_norm_g[l], ATT_KV_HEADS).reshape(1, ATT_KV_WIDTH)
        gw = gmlp_w[l].transpose(1, 0, 2).reshape(CHUNK, A_HEADS * CHUNK).astype(BF16)
        gb = jnp.repeat(gmlp_b[l].T, A_HEAD_DIM, axis=1)
        cw = conv_c_w[l]
        shared = (w_in_b, qg, kg)
        score_bound = (1.02 * ATT_HEAD_DIM * ATT_SCALE * LOG2_E
                       * jnp.max(jnp.abs(q_norm_g[l])) * jnp.max(jnp.abs(k_norm_g[l])))
        bounded = score_bound <= SCORE_BOUND_LOG2
        tail = (jmat, gw, gb, cw)

        q_c, k_c, v_c, a_c, cm_c = _inproj_call(
            xc, mod_ctx, g1, *shared, cos_c, sin_c, *tail, tm=tm_ctx)
        if not last:
            att_c = _attention(q_c, (k_c, v_c), bounded)
            xc = _outproj_call(xc, a_c, att_c, cm_c, mod_ctx, w_out_b, tm_ctx)
            xc = _ffn_call(xc, mod_ctx, g2, wu, wc, wd, final_row, tm_ctx, False)

        q_l, k_l, v_l, a_l, cm_l = _inproj_call(
            x, mod_lat, g1, *shared, cos_l, sin_l, *tail, tm=tm_lat)
        att = _attention(q_l, (k_c, v_c, k_l, v_l), bounded)
        x = _outproj_call(x, a_l, att, cm_l, mod_lat, w_out_b, 1024)
        x = _ffn_call(x, mod_lat, g2, wu, wc, wd, final_row, tm_lat, last)
    return x
```

```python
import functools

import jax
import jax.numpy as jnp
from jax import lax
from jax.experimental import pallas as pl
from jax.experimental.pallas import tpu as pltpu

F32 = jnp.float32
BF16 = jnp.bfloat16

EPS = 1e-6
ROPE_THETA = 10000.0
GRID_W = 64
CHUNK = 128
N_MOD = 6

A_HEADS = 4
A_HEAD_DIM = 64
A_WIDTH = A_HEADS * A_HEAD_DIM
ATT_Q_HEADS = 8
ATT_KV_HEADS = 2
ATT_HEAD_DIM = 64
ATT_REP = ATT_Q_HEADS // ATT_KV_HEADS
ATT_WIDTH = ATT_Q_HEADS * ATT_HEAD_DIM
ATT_KV_WIDTH = ATT_KV_HEADS * ATT_HEAD_DIM
ATT_SCALE = ATT_HEAD_DIM ** -0.5
LOG2_E = 1.4426950408889634
ROPE_AXIS_FREQS = ATT_HEAD_DIM // 4
C_WIDTH = 256

OFF_AU = 0
OFF_AV = OFF_AU + A_WIDTH
OFF_Q = OFF_AV + A_WIDTH
OFF_K = OFF_Q + ATT_WIDTH
OFF_V = OFF_K + ATT_KV_WIDTH
OFF_CB = OFF_V + ATT_KV_WIDTH
OFF_CC = OFF_CB + C_WIDTH
OFF_CH = OFF_CC + C_WIDTH
D_IN = OFF_CH + C_WIDTH

V7X_SUBLANES = 8
V7X_LANES = 128
V7X_VMEM_BYTES = 64 * 1024 * 1024
HALO = V7X_SUBLANES
BF16_ROWS = 16
Q_POS = V7X_LANES
SOFTMAX_ROWS = ATT_HEAD_DIM + BF16_ROWS
NEG_BIG = -1e30
ATTN_LOOKAHEAD = 1
FFN_CHUNK = 256
ATTN_TILES = 4
SCORE_BOUND_LOG2 = 50.0


def _vmem_limit(nbytes):
    return int(min(max(nbytes, 16 * 1024 * 1024), V7X_VMEM_BYTES - 6 * 1024 * 1024))


def _sigmoid(x):
    return 1.0 / (1.0 + jnp.exp(-x))


def _gelu_tanh(x):
    c = 0.7978845608028654
    return x * (0.5 * (1.0 + jnp.tanh(c * (x + 0.044715 * (x * x * x)))))


def _split_dot(x, j):
    hi = x.astype(BF16)
    lo = (x - hi.astype(F32)).astype(BF16)
    return (jnp.dot(hi, j, preferred_element_type=F32)
            + jnp.dot(lo, j, preferred_element_type=F32))


def _norm_modulate(x, g, shift, scale):
    ms = jnp.mean(x * x, axis=-1, keepdims=True)
    return (x * lax.rsqrt(ms + EPS) * g) * (1.0 + scale) + shift


def _conv3_seq(u_all, w3, tm):
    rows = u_all.shape[0]
    dn = pltpu.roll(u_all, 1, 0)[HALO:HALO + tm]
    up = pltpu.roll(u_all, rows - 1, 0)[HALO:HALO + tm]
    return dn * w3[0:1, :] + u_all[HALO:HALO + tm] * w3[1:2, :] + up * w3[2:3, :]


def _halo_norm_modulate(x_ref, xp_ref, xn_ref, g, shift, scale):
    i = pl.program_id(1)
    has_prev = (i > 0).astype(F32)
    has_next = (i < pl.num_programs(1) - 1).astype(F32)
    hm = _norm_modulate(x_ref[0], g, shift, scale)
    hp = _norm_modulate(xp_ref[0], g, shift, scale) * has_prev
    hn = _norm_modulate(xn_ref[0], g, shift, scale) * has_next
    return hm.astype(BF16), jnp.concatenate([hp, hm, hn], axis=0).astype(BF16)


def _rope128(x, cos_t, sin_t, odd16):
    sw = jnp.where(odd16, pltpu.roll(x, 16, 1), pltpu.roll(x, V7X_LANES - 16, 1))
    return x * cos_t + sw * sin_t


def _layer_spec(arr, l):
    nd = arr.ndim
    return pl.BlockSpec((1,) + arr.shape[1:], lambda b, i: (l,) + (0,) * (nd - 1),
                        pipeline_mode=pl.Buffered(1))


def _mod_spec(mod, l, row):
    blk = (1, 1) + mod.shape[2:]
    if row is None:
        return pl.BlockSpec(blk, lambda b, i: (l, b, 0, 0))
    return pl.BlockSpec(blk, lambda b, i: (l, row, 0, 0))


def _cast_kernel(w_ref, o_ref):
    o_ref[...] = w_ref[...].astype(o_ref.dtype)


def _cast_stack(w, rows):
    depth, r, c = w.shape
    spec = pl.BlockSpec((1, rows, c), lambda l, i: (l, i, 0))
    return pl.pallas_call(
        _cast_kernel,
        grid=(depth, r // rows),
        in_specs=[spec],
        out_specs=spec,
        out_shape=jax.ShapeDtypeStruct(w.shape, BF16),
        compiler_params=pltpu.CompilerParams(
            dimension_semantics=("arbitrary", "arbitrary"),
            vmem_limit_bytes=_vmem_limit(6 * rows * c * 4)),
        name="cast_bf16",
    )(w)


def _cast_gate_kernel(a_ref, g_ref, o_ref, *, cf):
    for j in range(a_ref.shape[2] // cf):
        o_ref[0, :, 2 * j * cf:(2 * j + 1) * cf] = a_ref[0, :, j * cf:(j + 1) * cf].astype(o_ref.dtype)
        o_ref[0, :, (2 * j + 1) * cf:(2 * j + 2) * cf] = g_ref[0, :, j * cf:(j + 1) * cf].astype(o_ref.dtype)


def _cast_gate_stack(w, cf, rows):
    depth, r, c2 = w.shape
    d_ff = c2 // 2
    return pl.pallas_call(
        functools.partial(_cast_gate_kernel, cf=cf),
        grid=(depth, r // rows),
        in_specs=[pl.BlockSpec((1, rows, d_ff), lambda l, i: (l, i, 0)),
                  pl.BlockSpec((1, rows, d_ff), lambda l, i: (l, i, 1))],
        out_specs=pl.BlockSpec((1, rows, c2), lambda l, i: (l, i, 0)),
        out_shape=jax.ShapeDtypeStruct(w.shape, BF16),
        compiler_params=pltpu.CompilerParams(
            dimension_semantics=("arbitrary", "arbitrary"),
            vmem_limit_bytes=_vmem_limit(8 * rows * c2 * 4)),
        name="cast_gate_bf16",
    )(w, w)


def _mod_kernel(c_ref, w_ref, b_ref, o_ref):
    c = c_ref[...]
    s = (c * _sigmoid(c)).astype(BF16)
    o_ref[0] = jnp.dot(s, w_ref[0].astype(BF16), preferred_element_type=F32) + b_ref[0]


def _mod_call(c_rows, w_mod, b_mod):
    depth, d, nm = w_mod.shape
    bn = 1536
    return pl.pallas_call(
        _mod_kernel,
        grid=(depth, nm // bn),
        in_specs=[
            pl.BlockSpec((V7X_SUBLANES, d), lambda l, j: (0, 0)),
            pl.BlockSpec((1, d, bn), lambda l, j: (l, 0, j)),
            pl.BlockSpec((1, 1, bn), lambda l, j: (l, 0, j)),
        ],
        out_specs=pl.BlockSpec((1, V7X_SUBLANES, bn), lambda l, j: (l, 0, j)),
        out_shape=jax.ShapeDtypeStruct((depth, V7X_SUBLANES, nm), F32),
        compiler_params=pltpu.CompilerParams(
            dimension_semantics=("arbitrary", "arbitrary"),
            vmem_limit_bytes=_vmem_limit(3 * d * bn * 4)),
        name="adaln_mod",
    )(c_rows, w_mod, b_mod.reshape(depth, 1, nm))


def _inproj_kernel(x_ref, xp_ref, xn_ref, mod_ref, g_ref, w_ref, qg_ref, kg_ref,
                   cos_ref, sin_ref, j_ref, gw_ref, gb_ref, cw_ref,
                   q_ref, k_ref, v_ref, a_ref, cm_ref, *, tm):
    mod = mod_ref[0, 0]
    h, h_all = _halo_norm_modulate(x_ref, xp_ref, xn_ref, g_ref[0], mod[0:1, :], mod[1:2, :])
    proj = jnp.dot(h, w_ref[0, :, :OFF_CC], preferred_element_type=F32)
    pch = jnp.dot(h_all, w_ref[0, :, OFF_CC:], preferred_element_type=F32)

    cos_t = cos_ref[...]
    sin_t = sin_ref[...]
    lane = lax.broadcasted_iota(jnp.int32, (tm, V7X_LANES), 1)
    odd16 = (lane & 16) != 0
    inv_hd = 1.0 / ATT_HEAD_DIM

    q = proj[:, OFF_Q:OFF_K]
    qn = q * lax.rsqrt(_split_dot(q * q, j_ref[...]) * inv_hd + EPS) * qg_ref[0]
    heads_per_blk = V7X_LANES // ATT_HEAD_DIM
    for t in range(ATT_WIDTH // V7X_LANES):
        blk = _rope128(qn[:, t * V7X_LANES:(t + 1) * V7X_LANES], cos_t, sin_t, odd16)
        blk_t = (blk * (ATT_SCALE * LOG2_E)).T.astype(BF16)
        for u in range(heads_per_blk):
            gi, ri = divmod(t * heads_per_blk + u, ATT_REP)
            for s in range(tm // Q_POS):
                col = s * ATT_REP * Q_POS + ri * Q_POS
                q_ref[0, gi, :, col:col + Q_POS] = blk_t[u * ATT_HEAD_DIM:(u + 1) * ATT_HEAD_DIM,
                                                         s * Q_POS:(s + 1) * Q_POS]
    k = proj[:, OFF_K:OFF_V]
    kn = k * lax.rsqrt(_split_dot(k * k, j_ref[:ATT_KV_WIDTH, :ATT_KV_WIDTH]) * inv_hd + EPS) * kg_ref[0]
    k_ref[0] = _rope128(kn, cos_t, sin_t, odd16).astype(BF16)
    v_ref[0] = proj[:, OFF_V:OFF_CB].T.astype(BF16)

    u_act = _gelu_tanh(proj[:, OFF_AU:OFF_AV])
    v_act = _gelu_tanh(proj[:, OFF_AV:OFF_Q])
    jl = j_ref[:A_WIDTH, :A_WIDTH]
    mu = _split_dot(v_act, jl) * (1.0 / A_HEAD_DIM)
    xc = v_act - mu
    var = _split_dot(xc * xc, jl) * (1.0 / A_HEAD_DIM)
    vln = xc * lax.rsqrt(var + EPS)
    lane_a = lax.broadcasted_iota(jnp.int32, (CHUNK, A_WIDTH), 1)
    for c in range(tm // CHUNK):
        vc = vln[c * CHUNK:(c + 1) * CHUNK]
        vs = jnp.concatenate(
            [jnp.where((lane_a // A_HEAD_DIM) == hh, vc, 0.0).astype(BF16) for hh in range(A_HEADS)],
            axis=0)
        mixed = jnp.dot(gw_ref[0], vs, preferred_element_type=F32) + gb_ref[0]
        a_ref[0, c * CHUNK:(c + 1) * CHUNK, :] = (u_act[c * CHUNK:(c + 1) * CHUNK] * mixed).astype(BF16)

    ch = pch[:, :C_WIDTH] * pch[:, C_WIDTH:]
    cm = proj[:, OFF_CB:OFF_CC] * _conv3_seq(ch, cw_ref[0], tm)
    cm_ref[0] = cm.astype(BF16)


def _row_tile_specs(tm, n, d):
    nb8 = n // HALO
    r = tm // HALO
    return [
        pl.BlockSpec((1, tm, d), lambda b, i: (b, i, 0)),
        pl.BlockSpec((1, HALO, d), lambda b, i: (b, jnp.maximum(i * r - 1, 0), 0)),
        pl.BlockSpec((1, HALO, d), lambda b, i: (b, jnp.minimum((i + 1) * r, nb8 - 1), 0)),
    ]


def _const_spec(shape):
    nd = len(shape)
    return pl.BlockSpec(shape, lambda b, i: (0,) * nd, pipeline_mode=pl.Buffered(1))


def _inproj_call(x, l, mod, mod_row, g1, w_in, qg, kg, cos_t, sin_t, jmat, gw, gb, cw, tm):
    bsz, n, d = x.shape
    kern = functools.partial(_inproj_kernel, tm=tm)
    row = lambda w: pl.BlockSpec((1, tm, w), lambda b, i: (b, i, 0))
    est = (4 * tm * d * 4 + d * D_IN * 2 + 8 * tm * D_IN * 4 + 4 * tm * 2048 * 2)
    return pl.pallas_call(
        kern,
        grid=(bsz, n // tm),
        in_specs=_row_tile_specs(tm, n, d) + [
            _mod_spec(mod, l, mod_row),
            _layer_spec(g1, l),
            _layer_spec(w_in, l),
            _layer_spec(qg, l),
            _layer_spec(kg, l),
            pl.BlockSpec((tm, V7X_LANES), lambda b, i: (i, 0)),
            pl.BlockSpec((tm, V7X_LANES), lambda b, i: (i, 0)),
            _const_spec(jmat.shape),
            _layer_spec(gw, l),
            _layer_spec(gb, l),
            _layer_spec(cw, l),
        ],
        out_specs=[
            pl.BlockSpec((1, ATT_KV_HEADS, ATT_HEAD_DIM, ATT_REP * tm), lambda b, i: (b, 0, 0, i)),
            row(ATT_KV_WIDTH),
            pl.BlockSpec((1, ATT_KV_WIDTH, tm), lambda b, i: (b, 0, i)),
            row(A_WIDTH), row(C_WIDTH)],
        out_shape=[
            jax.ShapeDtypeStruct((bsz, ATT_KV_HEADS, ATT_HEAD_DIM, ATT_REP * n), BF16),
            jax.ShapeDtypeStruct((bsz, n, ATT_KV_WIDTH), BF16),
            jax.ShapeDtypeStruct((bsz, ATT_KV_WIDTH, n), BF16),
            jax.ShapeDtypeStruct((bsz, n, A_WIDTH), BF16),
            jax.ShapeDtypeStruct((bsz, n, C_WIDTH), BF16),
        ],
        compiler_params=pltpu.CompilerParams(
            dimension_semantics=("arbitrary", "arbitrary"),
            vmem_limit_bytes=_vmem_limit(est)),
        name="inproj",
    )(x, x, x, mod, g1, w_in, qg, kg, cos_t, sin_t, jmat, gw, gb, cw)


def _attn_operands(qt_ref, kv_refs, tk, t):
    g = pl.program_id(1)
    tq = ATT_REP * Q_POS
    qt = qt_ref[0, 0, :, t * tq:(t + 1) * tq]
    zero = jnp.zeros_like(qt)
    qpad = jnp.concatenate([jnp.where(g == gg, qt, zero) for gg in range(ATT_KV_HEADS)], axis=0)
    ones_rows = (lax.broadcasted_iota(jnp.int32, (BF16_ROWS, tk), 0) == 0).astype(BF16)
    k_refs, vt_refs = kv_refs[0::2], kv_refs[1::2]
    starts = [0]
    for kr in k_refs:
        starts.append(starts[-1] + kr.shape[1])

    def pieces(j):
        lo, hi = j * tk, (j + 1) * tk
        return [(s, max(lo, starts[s]) - starts[s], min(hi, starts[s + 1]) - starts[s])
                for s in range(len(k_refs)) if max(lo, starts[s]) < min(hi, starts[s + 1])]

    def cat(parts, axis):
        return parts[0] if len(parts) == 1 else jnp.concatenate(parts, axis=axis)

    def scores(j):
        kb = cat([k_refs[s][0, a:b, :] for s, a, b in pieces(j)], 0)
        return jnp.dot(kb, qpad, preferred_element_type=F32)

    def values(j):
        vb = cat([vt_refs[s][0, :, a:b] for s, a, b in pieces(j)], 1)
        return jnp.concatenate([vb, ones_rows], axis=0)

    return scores, values, starts[-1] // tk


def _attn_finish(acc, o_ref, t):
    o = acc[:ATT_HEAD_DIM] / acc[ATT_HEAD_DIM:ATT_HEAD_DIM + 1]
    heads_per_blk = V7X_LANES // ATT_HEAD_DIM
    for p in range(ATT_REP // heads_per_blk):
        blk = jnp.concatenate(
            [o[:, (p * heads_per_blk + u) * Q_POS:(p * heads_per_blk + u + 1) * Q_POS]
             for u in range(heads_per_blk)], axis=0)
        o_ref[0, t * Q_POS:(t + 1) * Q_POS, p * V7X_LANES:(p + 1) * V7X_LANES] = blk.T.astype(o_ref.dtype)


def _attn_bounded_kernel(qt_ref, *refs, tk):
    *kv_refs, o_ref = refs
    for t in range(qt_ref.shape[3] // (ATT_REP * Q_POS)):
        scores, values, nkb = _attn_operands(qt_ref, kv_refs, tk, t)
        acc = None
        ahead = [jnp.exp2(scores(j)).astype(BF16) for j in range(min(ATTN_LOOKAHEAD, nkb))]
        for j in range(nkb):
            p = ahead.pop(0)
            if j + ATTN_LOOKAHEAD < nkb:
                ahead.append(jnp.exp2(scores(j + ATTN_LOOKAHEAD)).astype(BF16))
            part = jnp.dot(values(j), p, preferred_element_type=F32)
            acc = part if acc is None else acc + part
        _attn_finish(acc, o_ref, t)


def _attn_online_kernel(qt_ref, *refs, tk):
    *kv_refs, o_ref = refs
    tq = ATT_REP * Q_POS
    for t in range(qt_ref.shape[3] // tq):
        scores, values, nkb = _attn_operands(qt_ref, kv_refs, tk, t)
        m = jnp.full((1, tq), NEG_BIG, F32)
        acc = jnp.zeros((SOFTMAX_ROWS, tq), F32)
        for j in range(nkb):
            st = scores(j)
            m_new = jnp.maximum(m, jnp.max(st, axis=0, keepdims=True))
            p = jnp.exp2(st - m_new).astype(BF16)
            acc = jnp.exp2(m - m_new) * acc + jnp.dot(values(j), p, preferred_element_type=F32)
            m = m_new
        _attn_finish(acc, o_ref, t)


def _attn_call(body, name, qt, *kv):
    bsz, g, hd, nq = qt.shape
    nk = sum(k.shape[1] for k in kv[0::2])
    n = nq // ATT_REP
    tiles = max(t for t in range(1, ATTN_TILES + 1) if n % (t * Q_POS) == 0)
    tq = tiles * ATT_REP * Q_POS
    tk = _key_block(nk)
    kern = functools.partial(body, tk=tk)
    est = 4 * nk * V7X_LANES * 2 + 4 * hd * nk * 2 + 10 * tk * ATT_REP * Q_POS * 4
    kv_specs = []
    for k, vt in zip(kv[0::2], kv[1::2]):
        kv_specs.append(pl.BlockSpec((1, k.shape[1], ATT_KV_WIDTH), lambda b, gg, i: (b, 0, 0)))
        kv_specs.append(pl.BlockSpec((1, hd, vt.shape[2]), lambda b, gg, i: (b, gg, 0)))
    return pl.pallas_call(
        kern,
        grid=(bsz, g, nq // tq),
        in_specs=[pl.BlockSpec((1, 1, hd, tq), lambda b, gg, i: (b, gg, 0, i))] + kv_specs,
        out_specs=pl.BlockSpec((1, tiles * Q_POS, ATT_REP * hd), lambda b, gg, i: (b, i, gg)),
        out_shape=jax.ShapeDtypeStruct((bsz, n, ATT_WIDTH), BF16),
        compiler_params=pltpu.CompilerParams(
            dimension_semantics=("arbitrary", "arbitrary", "arbitrary"),
            vmem_limit_bytes=_vmem_limit(est)),
        name=name,
    )(qt, *kv)


def _key_block(nk):
    for tk in (768, 512, 256, 128):
        if nk % tk == 0:
            return tk
    raise ValueError(f"unsupported key count {nk}")


def _attention(qt, kv, bounded):
    return lax.cond(
        bounded,
        functools.partial(_attn_call, _attn_bounded_kernel, "flash_attn_bounded"),
        functools.partial(_attn_call, _attn_online_kernel, "flash_attn_online"),
        qt, *kv)


def _outproj_kernel(x_ref, a_ref, att_ref, cm_ref, mod_ref, w_ref, o_ref):
    mix = (jnp.dot(a_ref[0], w_ref[0, :A_WIDTH, :], preferred_element_type=F32)
           + jnp.dot(att_ref[0], w_ref[0, A_WIDTH:A_WIDTH + ATT_WIDTH, :], preferred_element_type=F32)
           + jnp.dot(cm_ref[0], w_ref[0, A_WIDTH + ATT_WIDTH:, :], preferred_element_type=F32))
    o_ref[0] = x_ref[0] + mod_ref[0, 0][2:3, :] * mix


def _outproj_call(x, a, att, cm, l, mod, mod_row, w_out, tm):
    bsz, n, d = x.shape
    row = lambda w: pl.BlockSpec((1, tm, w), lambda b, i: (b, i, 0))
    est = 6 * tm * d * 4 + d * d * 2 + 4 * tm * d * 2
    return pl.pallas_call(
        _outproj_kernel,
        grid=(bsz, n // tm),
        in_specs=[row(d), row(A_WIDTH), row(ATT_WIDTH), row(C_WIDTH),
                  _mod_spec(mod, l, mod_row),
                  _layer_spec(w_out, l)],
        out_specs=row(d),
        out_shape=jax.ShapeDtypeStruct((bsz, n, d), F32),
        compiler_params=pltpu.CompilerParams(
            dimension_semantics=("arbitrary", "arbitrary"),
            vmem_limit_bytes=_vmem_limit(est)),
        name="outproj",
    )(x, a, att, cm, mod, w_out)


def _ffn_kernel(x_ref, xp_ref, xn_ref, mod_ref, g_ref, wu_ref, wc_ref, wd_ref, fg_ref,
                o_ref, *, tm, nchunk, cf, final_norm):
    mod = mod_ref[0, 0]
    _, h = _halo_norm_modulate(x_ref, xp_ref, xn_ref, g_ref[0], mod[3:4, :], mod[4:5, :])
    up_all = jnp.dot(h, wu_ref[0], preferred_element_type=F32)
    acts = []
    for j in range(nchunk):
        cv = _conv3_seq(up_all[:, j * 2 * cf:(j + 1) * 2 * cf], wc_ref[0, :, j * 2 * cf:(j + 1) * 2 * cf], tm)
        a_ = cv[:, :cf]
        g_ = cv[:, cf:]
        acts.append(((g_ * _sigmoid(g_)) * a_).astype(BF16))
    act = jnp.concatenate(acts, axis=1)
    y = x_ref[0] + mod[5:6, :] * jnp.dot(act, wd_ref[0], preferred_element_type=F32)
    if final_norm:
        ms = jnp.mean(y * y, axis=-1, keepdims=True)
        y = y * lax.rsqrt(ms + EPS) * fg_ref[...]
    o_ref[0] = y


def _ffn_call(x, l, mod, mod_row, g2, wu, wc, wd, fg, tm, final_norm):
    bsz, n, d = x.shape
    d_ff = wd.shape[1]
    cf = FFN_CHUNK
    nchunk = d_ff // cf
    kern = functools.partial(_ffn_kernel, tm=tm, nchunk=nchunk, cf=cf, final_norm=final_norm)
    est = 3 * d * d_ff * 2 + 10 * tm * d * 4 + (tm + 2 * HALO) * 2 * d_ff * 4 + 2 * tm * d_ff * 2
    return pl.pallas_call(
        kern,
        grid=(bsz, n // tm),
        in_specs=_row_tile_specs(tm, n, d) + [
            _mod_spec(mod, l, mod_row),
            _layer_spec(g2, l),
            _layer_spec(wu, l),
            _layer_spec(wc, l),
            _layer_spec(wd, l),
            _const_spec((1, d)),
        ],
        out_specs=pl.BlockSpec((1, tm, d), lambda b, i: (b, i, 0)),
        out_shape=jax.ShapeDtypeStruct((bsz, n, d), F32),
        compiler_params=pltpu.CompilerParams(
            dimension_semantics=("arbitrary", "arbitrary"),
            vmem_limit_bytes=_vmem_limit(est)),
        name="convffn",
    )(x, x, x, mod, g2, wu, wc, wd, fg)


def _rope_tables(n):
    rows = n // GRID_W
    row = jnp.repeat(jnp.arange(rows), GRID_W).astype(F32)
    col = jnp.tile(jnp.arange(GRID_W), rows).astype(F32)
    inv = ROPE_THETA ** (-2.0 * jnp.arange(ROPE_AXIS_FREQS, dtype=F32) / (2 * ROPE_AXIS_FREQS))
    ar = row[:, None] * inv
    ac = col[:, None] * inv
    cos_h = jnp.concatenate([jnp.cos(ar), jnp.cos(ar), jnp.cos(ac), jnp.cos(ac)], axis=1)
    sin_h = jnp.concatenate([-jnp.sin(ar), jnp.sin(ar), -jnp.sin(ac), jnp.sin(ac)], axis=1)
    return jnp.tile(cos_h, (1, 2)), jnp.tile(sin_h, (1, 2))


def _block_ones(width, block):
    idx = jnp.arange(width) // block
    return (idx[:, None] == idx[None, :]).astype(BF16)


def _interleave_gate(w, d_ff, cf):
    lead = w.shape[:-1]
    a = w[..., :d_ff].reshape(lead + (d_ff // cf, cf))
    g = w[..., d_ff:].reshape(lead + (d_ff // cf, cf))
    return jnp.concatenate([a, g], axis=-1).reshape(lead + (2 * d_ff,))


def kernel(x, c, ctx, c_ctx, w_mod, b_mod, norm1_g, w_in, q_norm_g, k_norm_g, gmlp_w, gmlp_b,
           conv_c_w, w_out, norm2_g, ffn_up, ffn_conv_w, ffn_down, final_g):
    bsz, n, d = x.shape
    n_ctx = ctx.shape[1]
    depth = w_mod.shape[0]
    d_ff = ffn_down.shape[1]
    assert n % 512 == 0 and n_ctx % CHUNK == 0 and bsz + 1 <= V7X_SUBLANES

    tm_lat, tm_ctx = 512, n_ctx

    c_rows = jnp.concatenate(
        [c, c_ctx[None, :], jnp.zeros((V7X_SUBLANES - bsz - 1, d), F32)], axis=0)
    mod = _mod_call(c_rows, w_mod, b_mod).reshape(depth, V7X_SUBLANES, N_MOD, d)
    ctx_row = bsz

    cos_l, sin_l = _rope_tables(n)
    cos_c = jnp.ones((n_ctx, V7X_LANES), F32)
    sin_c = jnp.zeros((n_ctx, V7X_LANES), F32)
    jmat = _block_ones(ATT_WIDTH, ATT_HEAD_DIM)
    final_row = final_g.reshape(1, d)

    w_in_b = _cast_stack(w_in, 512)
    w_out_b = _cast_stack(w_out, 512)
    wd_b = _cast_stack(ffn_down, d_ff // 4)
    wu_b = _cast_gate_stack(ffn_up, FFN_CHUNK, 256)
    wc_i = _interleave_gate(ffn_conv_w, d_ff, FFN_CHUNK)
    g1 = norm1_g.reshape(depth, 1, d)
    g2 = norm2_g.reshape(depth, 1, d)
    qg = jnp.tile(q_norm_g, (1, ATT_Q_HEADS)).reshape(depth, 1, ATT_WIDTH)
    kg = jnp.tile(k_norm_g, (1, ATT_KV_HEADS)).reshape(depth, 1, ATT_KV_WIDTH)
    gw = gmlp_w.transpose(0, 2, 1, 3).reshape(depth, CHUNK, A_HEADS * CHUNK).astype(BF16)
    gb = jnp.repeat(gmlp_b.transpose(0, 2, 1), A_HEAD_DIM, axis=2)
    score_bound = (1.02 * ATT_HEAD_DIM * ATT_SCALE * LOG2_E
                   * jnp.max(jnp.abs(q_norm_g), axis=1) * jnp.max(jnp.abs(k_norm_g), axis=1))
    bounded = score_bound <= SCORE_BOUND_LOG2
    inproj_params = (g1, w_in_b, qg, kg)
    inproj_tail = (jmat, gw, gb, conv_c_w)
    ffn_params = (g2, wu_b, wc_i, wd_b, final_row)

    xc = ctx
    for l in range(depth):
        last = l == depth - 1
        q_c, k_c, v_c, a_c, cm_c = _inproj_call(
            xc, l, mod, ctx_row, *inproj_params, cos_c, sin_c, *inproj_tail, tm=tm_ctx)
        if not last:
            att_c = _attention(q_c, (k_c, v_c), bounded[l])
            xc = _outproj_call(xc, a_c, att_c, cm_c, l, mod, ctx_row, w_out_b, tm_ctx)
            xc = _ffn_call(xc, l, mod, ctx_row, *ffn_params, tm_ctx, False)

        q_l, k_l, v_l, a_l, cm_l = _inproj_call(
            x, l, mod, None, *inproj_params, cos_l, sin_l, *inproj_tail, tm=tm_lat)
        att = _attention(q_l, (k_c, v_c, k_l, v_l), bounded[l])
        x = _outproj_call(x, a_l, att, cm_l, l, mod, None, w_out_b, 1024)
        x = _ffn_call(x, l, mod, None, *ffn_params, tm_lat, last)
    return x
```

```python
import functools

import jax
import jax.numpy as jnp
from jax import lax
from jax.experimental import pallas as pl
from jax.experimental.pallas import tpu as pltpu

F32 = jnp.float32
BF16 = jnp.bfloat16

EPS = 1e-6
ROPE_THETA = 10000.0
GRID_W = 64
CHUNK = 128
N_MOD = 6

A_HEADS = 4
A_HEAD_DIM = 64
A_WIDTH = A_HEADS * A_HEAD_DIM
ATT_Q_HEADS = 8
ATT_KV_HEADS = 2
ATT_HEAD_DIM = 64
ATT_REP = ATT_Q_HEADS // ATT_KV_HEADS
ATT_WIDTH = ATT_Q_HEADS * ATT_HEAD_DIM
ATT_KV_WIDTH = ATT_KV_HEADS * ATT_HEAD_DIM
ATT_SCALE = ATT_HEAD_DIM ** -0.5
LOG2_E = 1.4426950408889634
ROPE_AXIS_FREQS = ATT_HEAD_DIM // 4
C_WIDTH = 256

OFF_AU = 0
OFF_AV = OFF_AU + A_WIDTH
OFF_Q = OFF_AV + A_WIDTH
OFF_K = OFF_Q + ATT_WIDTH
OFF_V = OFF_K + ATT_KV_WIDTH
OFF_CB = OFF_V + ATT_KV_WIDTH
OFF_CC = OFF_CB + C_WIDTH
OFF_CH = OFF_CC + C_WIDTH
D_IN = OFF_CH + C_WIDTH

V7X_SUBLANES = 8
V7X_LANES = 128
V7X_VMEM_BYTES = 64 * 1024 * 1024
HALO = V7X_SUBLANES
BF16_ROWS = 16
Q_POS = V7X_LANES
SOFTMAX_ROWS = ATT_HEAD_DIM + BF16_ROWS
NEG_BIG = -1e30
ATTN_LOOKAHEAD = 1
FFN_CHUNK = 256
ATTN_TILES = 4
SCORE_BOUND_LOG2 = 50.0


def _vmem_limit(nbytes):
    return int(min(max(nbytes, 16 * 1024 * 1024), V7X_VMEM_BYTES - 6 * 1024 * 1024))


def _sigmoid(x):
    return 1.0 / (1.0 + jnp.exp(-x))


def _gelu_tanh(x):
    c = 0.7978845608028654
    return x * (0.5 * (1.0 + jnp.tanh(c * (x + 0.044715 * (x * x * x)))))


def _split_dot(x, j):
    hi = x.astype(BF16)
    lo = (x - hi.astype(F32)).astype(BF16)
    return (jnp.dot(hi, j, preferred_element_type=F32)
            + jnp.dot(lo, j, preferred_element_type=F32))


def _norm_modulate(x, g, shift, scale):
    ms = jnp.mean(x * x, axis=-1, keepdims=True)
    return (x * lax.rsqrt(ms + EPS) * g) * (1.0 + scale) + shift


def _conv3_seq(u_all, w3, tm):
    rows = u_all.shape[0]
    dn = pltpu.roll(u_all, 1, 0)[HALO:HALO + tm]
    up = pltpu.roll(u_all, rows - 1, 0)[HALO:HALO + tm]
    return dn * w3[0:1, :] + u_all[HALO:HALO + tm] * w3[1:2, :] + up * w3[2:3, :]


def _halo_norm_modulate(x_ref, xp_ref, xn_ref, g, shift, scale):
    i = pl.program_id(1)
    has_prev = (i > 0).astype(F32)
    has_next = (i < pl.num_programs(1) - 1).astype(F32)
    hm = _norm_modulate(x_ref[0], g, shift, scale)
    hp = _norm_modulate(xp_ref[0], g, shift, scale) * has_prev
    hn = _norm_modulate(xn_ref[0], g, shift, scale) * has_next
    return hm.astype(BF16), jnp.concatenate([hp, hm, hn], axis=0).astype(BF16)


def _rope128(x, cos_t, sin_t, odd16):
    sw = jnp.where(odd16, pltpu.roll(x, 16, 1), pltpu.roll(x, V7X_LANES - 16, 1))
    return x * cos_t + sw * sin_t


def _layer_spec(arr, l):
    nd = arr.ndim
    return pl.BlockSpec((1,) + arr.shape[1:], lambda b, i: (l,) + (0,) * (nd - 1),
                        pipeline_mode=pl.Buffered(1))


def _mod_spec(mod, l, row):
    blk = (1, 1) + mod.shape[2:]
    if row is None:
        return pl.BlockSpec(blk, lambda b, i: (l, b, 0, 0))
    return pl.BlockSpec(blk, lambda b, i: (l, row, 0, 0))


def _cast_kernel(w_ref, o_ref):
    o_ref[...] = w_ref[...].astype(o_ref.dtype)


def _cast_stack(w, rows):
    depth, r, c = w.shape
    spec = pl.BlockSpec((1, rows, c), lambda l, i: (l, i, 0))
    return pl.pallas_call(
        _cast_kernel,
        grid=(depth, r // rows),
        in_specs=[spec],
        out_specs=spec,
        out_shape=jax.ShapeDtypeStruct(w.shape, BF16),
        compiler_params=pltpu.CompilerParams(
            dimension_semantics=("arbitrary", "arbitrary"),
            vmem_limit_bytes=_vmem_limit(6 * rows * c * 4)),
        name="cast_bf16",
    )(w)


def _cast_gate_kernel(a_ref, g_ref, o_ref, *, cf):
    for j in range(a_ref.shape[2] // cf):
        o_ref[0, :, 2 * j * cf:(2 * j + 1) * cf] = a_ref[0, :, j * cf:(j + 1) * cf].astype(o_ref.dtype)
        o_ref[0, :, (2 * j + 1) * cf:(2 * j + 2) * cf] = g_ref[0, :, j * cf:(j + 1) * cf].astype(o_ref.dtype)


def _cast_gate_stack(w, cf, rows):
    depth, r, c2 = w.shape
    d_ff = c2 // 2
    return pl.pallas_call(
        functools.partial(_cast_gate_kernel, cf=cf),
        grid=(depth, r // rows),
        in_specs=[pl.BlockSpec((1, rows, d_ff), lambda l, i: (l, i, 0)),
                  pl.BlockSpec((1, rows, d_ff), lambda l, i: (l, i, 1))],
        out_specs=pl.BlockSpec((1, rows, c2), lambda l, i: (l, i, 0)),
        out_shape=jax.ShapeDtypeStruct(w.shape, BF16),
        compiler_params=pltpu.CompilerParams(
            dimension_semantics=("arbitrary", "arbitrary"),
            vmem_limit_bytes=_vmem_limit(8 * rows * c2 * 4)),
        name="cast_gate_bf16",
    )(w, w)


def _mod_kernel(c_ref, w_ref, b_ref, o_ref):
    c = c_ref[...]
    s = (c * _sigmoid(c)).astype(BF16)
    o_ref[0] = jnp.dot(s, w_ref[0].astype(BF16), preferred_element_type=F32) + b_ref[0]


def _mod_call(c_rows, w_mod, b_mod):
    depth, d, nm = w_mod.shape
    bn = 1536
    return pl.pallas_call(
        _mod_kernel,
        grid=(depth, nm // bn),
        in_specs=[
            pl.BlockSpec((V7X_SUBLANES, d), lambda l, j: (0, 0)),
            pl.BlockSpec((1, d, bn), lambda l, j: (l, 0, j)),
            pl.BlockSpec((1, 1, bn), lambda l, j: (l, 0, j)),
        ],
        out_specs=pl.BlockSpec((1, V7X_SUBLANES, bn), lambda l, j: (l, 0, j)),
        out_shape=jax.ShapeDtypeStruct((depth, V7X_SUBLANES, nm), F32),
        compiler_params=pltpu.CompilerParams(
            dimension_semantics=("arbitrary", "arbitrary"),
            vmem_limit_bytes=_vmem_limit(3 * d * bn * 4)),
        name="adaln_mod",
    )(c_rows, w_mod, b_mod.reshape(depth, 1, nm))


def _inproj_kernel(x_ref, xp_ref, xn_ref, mod_ref, g_ref, w_ref, qg_ref, kg_ref,
                   cos_ref, sin_ref, j_ref, gw_ref, gb_ref, cw_ref,
                   q_ref, k_ref, v_ref, a_ref, cm_ref, *, tm):
    mod = mod_ref[0, 0]
    h, h_all = _halo_norm_modulate(x_ref, xp_ref, xn_ref, g_ref[0], mod[0:1, :], mod[1:2, :])
    proj = jnp.dot(h, w_ref[0, :, :OFF_CC], preferred_element_type=F32)
    pch = jnp.dot(h_all, w_ref[0, :, OFF_CC:], preferred_element_type=F32)

    cos_t = cos_ref[...]
    sin_t = sin_ref[...]
    lane = lax.broadcasted_iota(jnp.int32, (tm, V7X_LANES), 1)
    odd16 = (lane & 16) != 0
    inv_hd = 1.0 / ATT_HEAD_DIM

    q = proj[:, OFF_Q:OFF_K]
    qn = q * lax.rsqrt(_split_dot(q * q, j_ref[...]) * inv_hd + EPS) * qg_ref[0]
    heads_per_blk = V7X_LANES // ATT_HEAD_DIM
    for t in range(ATT_WIDTH // V7X_LANES):
        blk = _rope128(qn[:, t * V7X_LANES:(t + 1) * V7X_LANES], cos_t, sin_t, odd16)
        blk_t = (blk * (ATT_SCALE * LOG2_E)).T.astype(BF16)
        for u in range(heads_per_blk):
            gi, ri = divmod(t * heads_per_blk + u, ATT_REP)
            for s in range(tm // Q_POS):
                col = s * ATT_REP * Q_POS + ri * Q_POS
                q_ref[0, gi, :, col:col + Q_POS] = blk_t[u * ATT_HEAD_DIM:(u + 1) * ATT_HEAD_DIM,
                                                         s * Q_POS:(s + 1) * Q_POS]
    k = proj[:, OFF_K:OFF_V]
    kn = k * lax.rsqrt(_split_dot(k * k, j_ref[:ATT_KV_WIDTH, :ATT_KV_WIDTH]) * inv_hd + EPS) * kg_ref[0]
    k_ref[0] = _rope128(kn, cos_t, sin_t, odd16).astype(BF16)
    v_ref[0] = proj[:, OFF_V:OFF_CB].T.astype(BF16)

    u_act = _gelu_tanh(proj[:, OFF_AU:OFF_AV])
    v_act = _gelu_tanh(proj[:, OFF_AV:OFF_Q])
    jl = j_ref[:A_WIDTH, :A_WIDTH]
    mu = _split_dot(v_act, jl) * (1.0 / A_HEAD_DIM)
    xc = v_act - mu
    var = _split_dot(xc * xc, jl) * (1.0 / A_HEAD_DIM)
    vln = xc * lax.rsqrt(var + EPS)
    lane_a = lax.broadcasted_iota(jnp.int32, (CHUNK, A_WIDTH), 1)
    for c in range(tm // CHUNK):
        vc = vln[c * CHUNK:(c + 1) * CHUNK]
        vs = jnp.concatenate(
            [jnp.where((lane_a // A_HEAD_DIM) == hh, vc, 0.0).astype(BF16) for hh in range(A_HEADS)],
            axis=0)
        mixed = jnp.dot(gw_ref[0], vs, preferred_element_type=F32) + gb_ref[0]
        a_ref[0, c * CHUNK:(c + 1) * CHUNK, :] = (u_act[c * CHUNK:(c + 1) * CHUNK] * mixed).astype(BF16)

    ch = pch[:, :C_WIDTH] * pch[:, C_WIDTH:]
    cm = proj[:, OFF_CB:OFF_CC] * _conv3_seq(ch, cw_ref[0], tm)
    cm_ref[0] = cm.astype(BF16)


def _row_tile_specs(tm, n, d):
    nb8 = n // HALO
    r = tm // HALO
    return [
        pl.BlockSpec((1, tm, d), lambda b, i: (b, i, 0)),
        pl.BlockSpec((1, HALO, d), lambda b, i: (b, jnp.maximum(i * r - 1, 0), 0)),
        pl.BlockSpec((1, HALO, d), lambda b, i: (b, jnp.minimum((i + 1) * r, nb8 - 1), 0)),
    ]


def _const_spec(shape):
    nd = len(shape)
    return pl.BlockSpec(shape, lambda b, i: (0,) * nd, pipeline_mode=pl.Buffered(1))


def _inproj_call(x, l, mod, mod_row, g1, w_in, qg, kg, cos_t, sin_t, jmat, gw, gb, cw, tm):
    bsz, n, d = x.shape
    kern = functools.partial(_inproj_kernel, tm=tm)
    row = lambda w: pl.BlockSpec((1, tm, w), lambda b, i: (b, i, 0))
    est = (4 * tm * d * 4 + d * D_IN * 2 + 8 * tm * D_IN * 4 + 4 * tm * 2048 * 2)
    return pl.pallas_call(
        kern,
        grid=(bsz, n // tm),
        in_specs=_row_tile_specs(tm, n, d) + [
            _mod_spec(mod, l, mod_row),
            _layer_spec(g1, l),
            _layer_spec(w_in, l),
            _layer_spec(qg, l),
            _layer_spec(kg, l),
            pl.BlockSpec((tm, V7X_LANES), lambda b, i: (i, 0)),
            pl.BlockSpec((tm, V7X_LANES), lambda b, i: (i, 0)),
            _const_spec(jmat.shape),
            _layer_spec(gw, l),
            _layer_spec(gb, l),
            _layer_spec(cw, l),
        ],
        out_specs=[
            pl.BlockSpec((1, ATT_KV_HEADS, ATT_HEAD_DIM, ATT_REP * tm), lambda b, i: (b, 0, 0, i)),
            row(ATT_KV_WIDTH),
            pl.BlockSpec((1, ATT_KV_WIDTH, tm), lambda b, i: (b, 0, i)),
            row(A_WIDTH), row(C_WIDTH)],
        out_shape=[
            jax.ShapeDtypeStruct((bsz, ATT_KV_HEADS, ATT_HEAD_DIM, ATT_REP * n), BF16),
            jax.ShapeDtypeStruct((bsz, n, ATT_KV_WIDTH), BF16),
            jax.ShapeDtypeStruct((bsz, ATT_KV_WIDTH, n), BF16),
            jax.ShapeDtypeStruct((bsz, n, A_WIDTH), BF16),
            jax.ShapeDtypeStruct((bsz, n, C_WIDTH), BF16),
        ],
        compiler_params=pltpu.CompilerParams(
            dimension_semantics=("arbitrary", "arbitrary"),
            vmem_limit_bytes=_vmem_limit(est)),
        name="inproj",
    )(x, x, x, mod, g1, w_in, qg, kg, cos_t, sin_t, jmat, gw, gb, cw)


def _attn_operands(qt_ref, kv_refs, tk, t):
    g = pl.program_id(1)
    tq = ATT_REP * Q_POS
    qt = qt_ref[0, 0, :, t * tq:(t + 1) * tq]
    zero = jnp.zeros_like(qt)
    qpad = jnp.concatenate([jnp.where(g == gg, qt, zero) for gg in range(ATT_KV_HEADS)], axis=0)
    ones_rows = (lax.broadcasted_iota(jnp.int32, (BF16_ROWS, tk), 0) == 0).astype(BF16)
    k_refs, vt_refs = kv_refs[0::2], kv_refs[1::2]
    starts = [0]
    for kr in k_refs:
        starts.append(starts[-1] + kr.shape[1])

    def pieces(j):
        lo, hi = j * tk, (j + 1) * tk
        return [(s, max(lo, starts[s]) - starts[s], min(hi, starts[s + 1]) - starts[s])
                for s in range(len(k_refs)) if max(lo, starts[s]) < min(hi, starts[s + 1])]

    def cat(parts, axis):
        return parts[0] if len(parts) == 1 else jnp.concatenate(parts, axis=axis)

    def scores(j):
        kb = cat([k_refs[s][0, a:b, :] for s, a, b in pieces(j)], 0)
        return jnp.dot(kb, qpad, preferred_element_type=F32)

    def values(j):
        vb = cat([vt_refs[s][0, :, a:b] for s, a, b in pieces(j)], 1)
        return jnp.concatenate([vb, ones_rows], axis=0)

    return scores, values, starts[-1] // tk


def _attn_finish(acc, o_ref, t):
    o = acc[:ATT_HEAD_DIM] / acc[ATT_HEAD_DIM:ATT_HEAD_DIM + 1]
    heads_per_blk = V7X_LANES // ATT_HEAD_DIM
    for p in range(ATT_REP // heads_per_blk):
        blk = jnp.concatenate(
            [o[:, (p * heads_per_blk + u) * Q_POS:(p * heads_per_blk + u + 1) * Q_POS]
             for u in range(heads_per_blk)], axis=0)
        o_ref[0, t * Q_POS:(t + 1) * Q_POS, p * V7X_LANES:(p + 1) * V7X_LANES] = blk.T.astype(o_ref.dtype)


def _attn_bounded_kernel(qt_ref, *refs, tk):
    *kv_refs, o_ref = refs
    for t in range(qt_ref.shape[3] // (ATT_REP * Q_POS)):
        scores, values, nkb = _attn_operands(qt_ref, kv_refs, tk, t)
        acc = None
        ahead = [jnp.exp2(scores(j)).astype(BF16) for j in range(min(ATTN_LOOKAHEAD, nkb))]
        for j in range(nkb):
            if j + ATTN_LOOKAHEAD < nkb:
                ahead.append(jnp.exp2(scores(j + ATTN_LOOKAHEAD)).astype(BF16))
            p = ahead.pop(0)
            part = jnp.dot(values(j), p, preferred_element_type=F32)
            acc = part if acc is None else acc + part
        _attn_finish(acc, o_ref, t)


def _attn_online_kernel(qt_ref, *refs, tk):
    *kv_refs, o_ref = refs
    tq = ATT_REP * Q_POS
    for t in range(qt_ref.shape[3] // tq):
        scores, values, nkb = _attn_operands(qt_ref, kv_refs, tk, t)
        m = jnp.full((1, tq), NEG_BIG, F32)
        acc = jnp.zeros((SOFTMAX_ROWS, tq), F32)
        for j in range(nkb):
            st = scores(j)
            m_new = jnp.maximum(m, jnp.max(st, axis=0, keepdims=True))
            p = jnp.exp2(st - m_new).astype(BF16)
            acc = jnp.exp2(m - m_new) * acc + jnp.dot(values(j), p, preferred_element_type=F32)
            m = m_new
        _attn_finish(acc, o_ref, t)


def _attn_call(body, name, qt, *kv):
    bsz, g, hd, nq = qt.shape
    nk = sum(k.shape[1] for k in kv[0::2])
    n = nq // ATT_REP
    tiles = max(t for t in range(1, ATTN_TILES + 1) if n % (t * Q_POS) == 0)
    tq = tiles * ATT_REP * Q_POS
    tk = _key_block(nk)
    kern = functools.partial(body, tk=tk)
    est = 4 * nk * V7X_LANES * 2 + 4 * hd * nk * 2 + 10 * tk * ATT_REP * Q_POS * 4
    kv_specs = []
    for k, vt in zip(kv[0::2], kv[1::2]):
        kv_specs.append(pl.BlockSpec((1, k.shape[1], ATT_KV_WIDTH), lambda b, gg, i: (b, 0, 0)))
        kv_specs.append(pl.BlockSpec((1, hd, vt.shape[2]), lambda b, gg, i: (b, gg, 0)))
    return pl.pallas_call(
        kern,
        grid=(bsz, g, nq // tq),
        in_specs=[pl.BlockSpec((1, 1, hd, tq), lambda b, gg, i: (b, gg, 0, i))] + kv_specs,
        out_specs=pl.BlockSpec((1, tiles * Q_POS, ATT_REP * hd), lambda b, gg, i: (b, i, gg)),
        out_shape=jax.ShapeDtypeStruct((bsz, n, ATT_WIDTH), BF16),
        compiler_params=pltpu.CompilerParams(
            dimension_semantics=("arbitrary", "arbitrary", "arbitrary"),
            vmem_limit_bytes=_vmem_limit(est)),
        name=name,
    )(qt, *kv)


def _key_block(nk):
    for tk in (2816, 768, 512, 256, 128):
        if nk % tk == 0:
            return tk
    raise ValueError(f"unsupported key count {nk}")


def _attention(qt, kv, bounded):
    return lax.cond(
        bounded,
        functools.partial(_attn_call, _attn_bounded_kernel, "flash_attn_bounded"),
        functools.partial(_attn_call, _attn_online_kernel, "flash_attn_online"),
        qt, *kv)


def _outproj_kernel(x_ref, a_ref, att_ref, cm_ref, mod_ref, w_ref, o_ref):
    mix = (jnp.dot(a_ref[0], w_ref[0, :A_WIDTH, :], preferred_element_type=F32)
           + jnp.dot(att_ref[0], w_ref[0, A_WIDTH:A_WIDTH + ATT_WIDTH, :], preferred_element_type=F32)
           + jnp.dot(cm_ref[0], w_ref[0, A_WIDTH + ATT_WIDTH:, :], preferred_element_type=F32))
    o_ref[0] = x_ref[0] + mod_ref[0, 0][2:3, :] * mix


def _outproj_call(x, a, att, cm, l, mod, mod_row, w_out, tm):
    bsz, n, d = x.shape
    row = lambda w: pl.BlockSpec((1, tm, w), lambda b, i: (b, i, 0))
    est = 6 * tm * d * 4 + d * d * 2 + 4 * tm * d * 2
    return pl.pallas_call(
        _outproj_kernel,
        grid=(bsz, n // tm),
        in_specs=[row(d), row(A_WIDTH), row(ATT_WIDTH), row(C_WIDTH),
                  _mod_spec(mod, l, mod_row),
                  _layer_spec(w_out, l)],
        out_specs=row(d),
        out_shape=jax.ShapeDtypeStruct((bsz, n, d), F32),
        compiler_params=pltpu.CompilerParams(
            dimension_semantics=("arbitrary", "arbitrary"),
            vmem_limit_bytes=_vmem_limit(est)),
        name="outproj",
    )(x, a, att, cm, mod, w_out)


def _ffn_kernel(x_ref, xp_ref, xn_ref, mod_ref, g_ref, wu_ref, wc_ref, wd_ref, fg_ref,
                o_ref, *, tm, nchunk, cf, final_norm):
    mod = mod_ref[0, 0]
    _, h = _halo_norm_modulate(x_ref, xp_ref, xn_ref, g_ref[0], mod[3:4, :], mod[4:5, :])
    up_all = jnp.dot(h, wu_ref[0], preferred_element_type=F32)
    acts = []
    for j in range(nchunk):
        cv = _conv3_seq(up_all[:, j * 2 * cf:(j + 1) * 2 * cf], wc_ref[0, :, j * 2 * cf:(j + 1) * 2 * cf], tm)
        a_ = cv[:, :cf]
        g_ = cv[:, cf:]
        acts.append(((g_ * _sigmoid(g_)) * a_).astype(BF16))
    act = jnp.concatenate(acts, axis=1)
    y = x_ref[0] + mod[5:6, :] * jnp.dot(act, wd_ref[0], preferred_element_type=F32)
    if final_norm:
        ms = jnp.mean(y * y, axis=-1, keepdims=True)
        y = y * lax.rsqrt(ms + EPS) * fg_ref[...]
    o_ref[0] = y


def _ffn_call(x, l, mod, mod_row, g2, wu, wc, wd, fg, tm, final_norm):
    bsz, n, d = x.shape
    d_ff = wd.shape[1]
    cf = FFN_CHUNK
    nchunk = d_ff // cf
    kern = functools.partial(_ffn_kernel, tm=tm, nchunk=nchunk, cf=cf, final_norm=final_norm)
    est = 3 * d * d_ff * 2 + 10 * tm * d * 4 + (tm + 2 * HALO) * 2 * d_ff * 4 + 2 * tm * d_ff * 2
    return pl.pallas_call(
        kern,
        grid=(bsz, n // tm),
        in_specs=_row_tile_specs(tm, n, d) + [
            _mod_spec(mod, l, mod_row),
            _layer_spec(g2, l),
            _layer_spec(wu, l),
            _layer_spec(wc, l),
            _layer_spec(wd, l),
            _const_spec((1, d)),
        ],
        out_specs=pl.BlockSpec((1, tm, d), lambda b, i: (b, i, 0)),
        out_shape=jax.ShapeDtypeStruct((bsz, n, d), F32),
        compiler_params=pltpu.CompilerParams(
            dimension_semantics=("arbitrary", "arbitrary"),
            vmem_limit_bytes=_vmem_limit(est)),
        name="convffn",
    )(x, x, x, mod, g2, wu, wc, wd, fg)


def _rope_tables(n):
    rows = n // GRID_W
    row = jnp.repeat(jnp.arange(rows), GRID_W).astype(F32)
    col = jnp.tile(jnp.arange(GRID_W), rows).astype(F32)
    inv = ROPE_THETA ** (-2.0 * jnp.arange(ROPE_AXIS_FREQS, dtype=F32) / (2 * ROPE_AXIS_FREQS))
    ar = row[:, None] * inv
    ac = col[:, None] * inv
    cos_h = jnp.concatenate([jnp.cos(ar), jnp.cos(ar), jnp.cos(ac), jnp.cos(ac)], axis=1)
    sin_h = jnp.concatenate([-jnp.sin(ar), jnp.sin(ar), -jnp.sin(ac), jnp.sin(ac)], axis=1)
    return jnp.tile(cos_h, (1, 2)), jnp.tile(sin_h, (1, 2))


def _block_ones(width, block):
    idx = jnp.arange(width) // block
    return (idx[:, None] == idx[None, :]).astype(BF16)


def _interleave_gate(w, d_ff, cf):
    lead = w.shape[:-1]
    a = w[..., :d_ff].reshape(lead + (d_ff // cf, cf))
    g = w[..., d_ff:].reshape(lead + (d_ff // cf, cf))
    return jnp.concatenate([a, g], axis=-1).reshape(lead + (2 * d_ff,))


def kernel(x, c, ctx, c_ctx, w_mod, b_mod, norm1_g, w_in, q_norm_g, k_norm_g, gmlp_w, gmlp_b,
           conv_c_w, w_out, norm2_g, ffn_up, ffn_conv_w, ffn_down, final_g):
    bsz, n, d = x.shape
    n_ctx = ctx.shape[1]
    depth = w_mod.shape[0]
    d_ff = ffn_down.shape[1]
    assert n % 512 == 0 and n_ctx % CHUNK == 0 and bsz + 1 <= V7X_SUBLANES

    tm_lat, tm_ctx = 512, n_ctx

    c_rows = jnp.concatenate(
        [c, c_ctx[None, :], jnp.zeros((V7X_SUBLANES - bsz - 1, d), F32)], axis=0)
    mod = _mod_call(c_rows, w_mod, b_mod).reshape(depth, V7X_SUBLANES, N_MOD, d)
    ctx_row = bsz

    cos_l, sin_l = _rope_tables(n)
    cos_c = jnp.ones((n_ctx, V7X_LANES), F32)
    sin_c = jnp.zeros((n_ctx, V7X_LANES), F32)
    jmat = _block_ones(ATT_WIDTH, ATT_HEAD_DIM)
    final_row = final_g.reshape(1, d)

    w_in_b = _cast_stack(w_in, 512)
    w_out_b = _cast_stack(w_out, 512)
    wd_b = _cast_stack(ffn_down, d_ff // 4)
    wu_b = _cast_gate_stack(ffn_up, FFN_CHUNK, 256)
    wc_i = _interleave_gate(ffn_conv_w, d_ff, FFN_CHUNK)
    g1 = norm1_g.reshape(depth, 1, d)
    g2 = norm2_g.reshape(depth, 1, d)
    qg = jnp.tile(q_norm_g, (1, ATT_Q_HEADS)).reshape(depth, 1, ATT_WIDTH)
    kg = jnp.tile(k_norm_g, (1, ATT_KV_HEADS)).reshape(depth, 1, ATT_KV_WIDTH)
    gw = gmlp_w.transpose(0, 2, 1, 3).reshape(depth, CHUNK, A_HEADS * CHUNK).astype(BF16)
    gb = jnp.repeat(gmlp_b.transpose(0, 2, 1), A_HEAD_DIM, axis=2)
    score_bound = (1.02 * ATT_HEAD_DIM * ATT_SCALE * LOG2_E
                   * jnp.max(jnp.abs(q_norm_g), axis=1) * jnp.max(jnp.abs(k_norm_g), axis=1))
    bounded = score_bound <= SCORE_BOUND_LOG2
    inproj_params = (g1, w_in_b, qg, kg)
    inproj_tail = (jmat, gw, gb, conv_c_w)
    ffn_params = (g2, wu_b, wc_i, wd_b, final_row)

    xc = ctx
    for l in range(depth):
        last = l == depth - 1
        q_c, k_c, v_c, a_c, cm_c = _inproj_call(
            xc, l, mod, ctx_row, *inproj_params, cos_c, sin_c, *inproj_tail, tm=tm_ctx)
        if not last:
            att_c = _attention(q_c, (k_c, v_c), bounded[l])
            xc = _outproj_call(xc, a_c, att_c, cm_c, l, mod, ctx_row, w_out_b, tm_ctx)
            xc = _ffn_call(xc, l, mod, ctx_row, *ffn_params, tm_ctx, False)

        q_l, k_l, v_l, a_l, cm_l = _inproj_call(
            x, l, mod, None, *inproj_params, cos_l, sin_l, *inproj_tail, tm=1024)
        att = _attention(q_l, (k_c, v_c, k_l, v_l), bounded[l])
        x = _outproj_call(x, a_l, att, cm_l, l, mod, None, w_out_b, 1024)
        x = _ffn_call(x, l, mod, None, *ffn_params, tm_lat, last)
    return x
```

```python
import functools

import jax
import jax.numpy as jnp
from jax import lax
from jax.experimental import pallas as pl
from jax.experimental.pallas import tpu as pltpu

F32 = jnp.float32
BF16 = jnp.bfloat16

EPS = 1e-6
ROPE_THETA = 10000.0
GRID_W = 64
CHUNK = 128
N_MOD = 6

A_HEADS = 4
A_HEAD_DIM = 64
A_WIDTH = A_HEADS * A_HEAD_DIM
ATT_Q_HEADS = 8
ATT_KV_HEADS = 2
ATT_HEAD_DIM = 64
ATT_REP = ATT_Q_HEADS // ATT_KV_HEADS
ATT_WIDTH = ATT_Q_HEADS * ATT_HEAD_DIM
ATT_KV_WIDTH = ATT_KV_HEADS * ATT_HEAD_DIM
ATT_SCALE = ATT_HEAD_DIM ** -0.5
LOG2_E = 1.4426950408889634
ROPE_AXIS_FREQS = ATT_HEAD_DIM // 4
C_WIDTH = 256

OFF_AU = 0
OFF_AV = OFF_AU + A_WIDTH
OFF_Q = OFF_AV + A_WIDTH
OFF_K = OFF_Q + ATT_WIDTH
OFF_V = OFF_K + ATT_KV_WIDTH
OFF_CB = OFF_V + ATT_KV_WIDTH
OFF_CC = OFF_CB + C_WIDTH
OFF_CH = OFF_CC + C_WIDTH
D_IN = OFF_CH + C_WIDTH

V7X_SUBLANES = 8
V7X_LANES = 128
V7X_VMEM_BYTES = 64 * 1024 * 1024
HALO = V7X_SUBLANES
BF16_ROWS = 16
Q_POS = V7X_LANES
SOFTMAX_ROWS = ATT_HEAD_DIM + BF16_ROWS
NEG_BIG = -1e30
ATTN_LOOKAHEAD = 3
FFN_CHUNK = 256
ATTN_TILES = 4
SCORE_BOUND_LOG2 = 50.0


def _vmem_limit(nbytes):
    return int(min(max(nbytes, 16 * 1024 * 1024), V7X_VMEM_BYTES - 6 * 1024 * 1024))


def _sigmoid(x):
    return 1.0 / (1.0 + jnp.exp(-x))


def _gelu_tanh(x):
    c = 0.7978845608028654
    return x * (0.5 * (1.0 + jnp.tanh(c * (x + 0.044715 * (x * x * x)))))


def _split_dot(x, j):
    hi = x.astype(BF16)
    lo = (x - hi.astype(F32)).astype(BF16)
    return (jnp.dot(hi, j, preferred_element_type=F32)
            + jnp.dot(lo, j, preferred_element_type=F32))


def _norm_modulate(x, g, shift, scale):
    ms = jnp.mean(x * x, axis=-1, keepdims=True)
    return (x * lax.rsqrt(ms + EPS) * g) * (1.0 + scale) + shift


def _conv3_seq(u_all, w3, tm):
    rows = u_all.shape[0]
    dn = pltpu.roll(u_all, 1, 0)[HALO:HALO + tm]
    up = pltpu.roll(u_all, rows - 1, 0)[HALO:HALO + tm]
    return dn * w3[0:1, :] + u_all[HALO:HALO + tm] * w3[1:2, :] + up * w3[2:3, :]


def _halo_norm_modulate(x_ref, xp_ref, xn_ref, g, shift, scale):
    i = pl.program_id(1)
    has_prev = (i > 0).astype(F32)
    has_next = (i < pl.num_programs(1) - 1).astype(F32)
    hm = _norm_modulate(x_ref[0], g, shift, scale)
    hp = _norm_modulate(xp_ref[0], g, shift, scale) * has_prev
    hn = _norm_modulate(xn_ref[0], g, shift, scale) * has_next
    return hm.astype(BF16), jnp.concatenate([hp, hm, hn], axis=0).astype(BF16)


def _rope128(x, cos_t, sin_t, odd16):
    sw = jnp.where(odd16, pltpu.roll(x, 16, 1), pltpu.roll(x, V7X_LANES - 16, 1))
    return x * cos_t + sw * sin_t


def _layer_spec(arr, l):
    nd = arr.ndim
    return pl.BlockSpec((1,) + arr.shape[1:], lambda b, i: (l,) + (0,) * (nd - 1),
                        pipeline_mode=pl.Buffered(1))


def _mod_spec(mod, l, row):
    blk = (1, 1) + mod.shape[2:]
    if row is None:
        return pl.BlockSpec(blk, lambda b, i: (l, b, 0, 0))
    return pl.BlockSpec(blk, lambda b, i: (l, row, 0, 0))


def _cast_kernel(w_ref, o_ref):
    o_ref[...] = w_ref[...].astype(o_ref.dtype)


def _cast_stack(w, rows):
    depth, r, c = w.shape
    spec = pl.BlockSpec((1, rows, c), lambda l, i: (l, i, 0))
    return pl.pallas_call(
        _cast_kernel,
        grid=(depth, r // rows),
        in_specs=[spec],
        out_specs=spec,
        out_shape=jax.ShapeDtypeStruct(w.shape, BF16),
        compiler_params=pltpu.CompilerParams(
            dimension_semantics=("arbitrary", "arbitrary"),
            vmem_limit_bytes=_vmem_limit(6 * rows * c * 4)),
        name="cast_bf16",
    )(w)


def _cast_gate_kernel(a_ref, g_ref, o_ref, *, cf):
    for j in range(a_ref.shape[2] // cf):
        o_ref[0, :, 2 * j * cf:(2 * j + 1) * cf] = a_ref[0, :, j * cf:(j + 1) * cf].astype(o_ref.dtype)
        o_ref[0, :, (2 * j + 1) * cf:(2 * j + 2) * cf] = g_ref[0, :, j * cf:(j + 1) * cf].astype(o_ref.dtype)


def _cast_gate_stack(w, cf, rows):
    depth, r, c2 = w.shape
    d_ff = c2 // 2
    return pl.pallas_call(
        functools.partial(_cast_gate_kernel, cf=cf),
        grid=(depth, r // rows),
        in_specs=[pl.BlockSpec((1, rows, d_ff), lambda l, i: (l, i, 0)),
                  pl.BlockSpec((1, rows, d_ff), lambda l, i: (l, i, 1))],
        out_specs=pl.BlockSpec((1, rows, c2), lambda l, i: (l, i, 0)),
        out_shape=jax.ShapeDtypeStruct(w.shape, BF16),
        compiler_params=pltpu.CompilerParams(
            dimension_semantics=("arbitrary", "arbitrary"),
            vmem_limit_bytes=_vmem_limit(8 * rows * c2 * 4)),
        name="cast_gate_bf16",
    )(w, w)


def _mod_kernel(c_ref, w_ref, b_ref, o_ref):
    c = c_ref[...]
    s = (c * _sigmoid(c)).astype(BF16)
    o_ref[0] = jnp.dot(s, w_ref[0].astype(BF16), preferred_element_type=F32) + b_ref[0]


def _mod_call(c_rows, w_mod, b_mod):
    depth, d, nm = w_mod.shape
    bn = 1536
    return pl.pallas_call(
        _mod_kernel,
        grid=(depth, nm // bn),
        in_specs=[
            pl.BlockSpec((V7X_SUBLANES, d), lambda l, j: (0, 0)),
            pl.BlockSpec((1, d, bn), lambda l, j: (l, 0, j)),
            pl.BlockSpec((1, 1, bn), lambda l, j: (l, 0, j)),
        ],
        out_specs=pl.BlockSpec((1, V7X_SUBLANES, bn), lambda l, j: (l, 0, j)),
        out_shape=jax.ShapeDtypeStruct((depth, V7X_SUBLANES, nm), F32),
        compiler_params=pltpu.CompilerParams(
            dimension_semantics=("arbitrary", "arbitrary"),
            vmem_limit_bytes=_vmem_limit(3 * d * bn * 4)),
        name="adaln_mod",
    )(c_rows, w_mod, b_mod.reshape(depth, 1, nm))


def _inproj_kernel(x_ref, xp_ref, xn_ref, mod_ref, g_ref, w_ref, qg_ref, kg_ref,
                   cos_ref, sin_ref, j_ref, gw_ref, gb_ref, cw_ref,
                   q_ref, k_ref, v_ref, a_ref, cm_ref, *, tm):
    mod = mod_ref[0, 0]
    h, h_all = _halo_norm_modulate(x_ref, xp_ref, xn_ref, g_ref[0], mod[0:1, :], mod[1:2, :])
    proj = jnp.dot(h, w_ref[0, :, :OFF_CC], preferred_element_type=F32)
    pch = jnp.dot(h_all, w_ref[0, :, OFF_CC:], preferred_element_type=F32)

    cos_t = cos_ref[...]
    sin_t = sin_ref[...]
    lane = lax.broadcasted_iota(jnp.int32, (tm, V7X_LANES), 1)
    odd16 = (lane & 16) != 0
    inv_hd = 1.0 / ATT_HEAD_DIM

    q = proj[:, OFF_Q:OFF_K]
    qn = q * lax.rsqrt(_split_dot(q * q, j_ref[...]) * inv_hd + EPS) * qg_ref[0]
    heads_per_blk = V7X_LANES // ATT_HEAD_DIM
    for t in range(ATT_WIDTH // V7X_LANES):
        blk = _rope128(qn[:, t * V7X_LANES:(t + 1) * V7X_LANES], cos_t, sin_t, odd16)
        blk_t = (blk * (ATT_SCALE * LOG2_E)).T.astype(BF16)
        for u in range(heads_per_blk):
            gi, ri = divmod(t * heads_per_blk + u, ATT_REP)
            for s in range(tm // Q_POS):
                col = s * ATT_REP * Q_POS + ri * Q_POS
                q_ref[0, gi, :, col:col + Q_POS] = blk_t[u * ATT_HEAD_DIM:(u + 1) * ATT_HEAD_DIM,
                                                         s * Q_POS:(s + 1) * Q_POS]
    k = proj[:, OFF_K:OFF_V]
    kn = k * lax.rsqrt(_split_dot(k * k, j_ref[:ATT_KV_WIDTH, :ATT_KV_WIDTH]) * inv_hd + EPS) * kg_ref[0]
    k_ref[0] = _rope128(kn, cos_t, sin_t, odd16).astype(BF16)
    v_ref[0] = proj[:, OFF_V:OFF_CB].T.astype(BF16)

    u_act = _gelu_tanh(proj[:, OFF_AU:OFF_AV])
    v_act = _gelu_tanh(proj[:, OFF_AV:OFF_Q])
    jl = j_ref[:A_WIDTH, :A_WIDTH]
    mu = _split_dot(v_act, jl) * (1.0 / A_HEAD_DIM)
    xc = v_act - mu
    var = _split_dot(xc * xc, jl) * (1.0 / A_HEAD_DIM)
    vln = xc * lax.rsqrt(var + EPS)
    lane_a = lax.broadcasted_iota(jnp.int32, (CHUNK, A_WIDTH), 1)
    for c in range(tm // CHUNK):
        vc = vln[c * CHUNK:(c + 1) * CHUNK]
        vs = jnp.concatenate(
            [jnp.where((lane_a // A_HEAD_DIM) == hh, vc, 0.0).astype(BF16) for hh in range(A_HEADS)],
            axis=0)
        mixed = jnp.dot(gw_ref[0], vs, preferred_element_type=F32) + gb_ref[0]
        a_ref[0, c * CHUNK:(c + 1) * CHUNK, :] = (u_act[c * CHUNK:(c + 1) * CHUNK] * mixed).astype(BF16)

    ch = pch[:, :C_WIDTH] * pch[:, C_WIDTH:]
    cm = proj[:, OFF_CB:OFF_CC] * _conv3_seq(ch, cw_ref[0], tm)
    cm_ref[0] = cm.astype(BF16)


def _row_tile_specs(tm, n, d):
    nb8 = n // HALO
    r = tm // HALO
    return [
        pl.BlockSpec((1, tm, d), lambda b, i: (b, i, 0)),
        pl.BlockSpec((1, HALO, d), lambda b, i: (b, jnp.maximum(i * r - 1, 0), 0)),
        pl.BlockSpec((1, HALO, d), lambda b, i: (b, jnp.minimum((i + 1) * r, nb8 - 1), 0)),
    ]


def _const_spec(shape):
    nd = len(shape)
    return pl.BlockSpec(shape, lambda b, i: (0,) * nd, pipeline_mode=pl.Buffered(1))


def _inproj_call(x, l, mod, mod_row, g1, w_in, qg, kg, cos_t, sin_t, jmat, gw, gb, cw, tm):
    bsz, n, d = x.shape
    kern = functools.partial(_inproj_kernel, tm=tm)
    row = lambda w: pl.BlockSpec((1, tm, w), lambda b, i: (b, i, 0))
    est = (4 * tm * d * 4 + d * D_IN * 2 + 8 * tm * D_IN * 4 + 4 * tm * 2048 * 2)
    return pl.pallas_call(
        kern,
        grid=(bsz, n // tm),
        in_specs=_row_tile_specs(tm, n, d) + [
            _mod_spec(mod, l, mod_row),
            _layer_spec(g1, l),
            _layer_spec(w_in, l),
            _layer_spec(qg, l),
            _layer_spec(kg, l),
            pl.BlockSpec((tm, V7X_LANES), lambda b, i: (i, 0)),
            pl.BlockSpec((tm, V7X_LANES), lambda b, i: (i, 0)),
            _const_spec(jmat.shape),
            _layer_spec(gw, l),
            _layer_spec(gb, l),
            _layer_spec(cw, l),
        ],
        out_specs=[
            pl.BlockSpec((1, ATT_KV_HEADS, ATT_HEAD_DIM, ATT_REP * tm), lambda b, i: (b, 0, 0, i)),
            row(ATT_KV_WIDTH),
            pl.BlockSpec((1, ATT_KV_WIDTH, tm), lambda b, i: (b, 0, i)),
            row(A_WIDTH), row(C_WIDTH)],
        out_shape=[
            jax.ShapeDtypeStruct((bsz, ATT_KV_HEADS, ATT_HEAD_DIM, ATT_REP * n), BF16),
            jax.ShapeDtypeStruct((bsz, n, ATT_KV_WIDTH), BF16),
            jax.ShapeDtypeStruct((bsz, ATT_KV_WIDTH, n), BF16),
            jax.ShapeDtypeStruct((bsz, n, A_WIDTH), BF16),
            jax.ShapeDtypeStruct((bsz, n, C_WIDTH), BF16),
        ],
        compiler_params=pltpu.CompilerParams(
            dimension_semantics=("arbitrary", "arbitrary"),
            vmem_limit_bytes=_vmem_limit(est)),
        name="inproj",
    )(x, x, x, mod, g1, w_in, qg, kg, cos_t, sin_t, jmat, gw, gb, cw)


def _attn_operands(qt_ref, kv_refs, tk, t):
    g = pl.program_id(1)
    tq = ATT_REP * Q_POS
    qt = qt_ref[0, 0, :, t * tq:(t + 1) * tq]
    zero = jnp.zeros_like(qt)
    qpad = jnp.concatenate([jnp.where(g == gg, qt, zero) for gg in range(ATT_KV_HEADS)], axis=0)
    ones_rows = (lax.broadcasted_iota(jnp.int32, (BF16_ROWS, tk), 0) == 0).astype(BF16)
    k_refs, vt_refs = kv_refs[0::2], kv_refs[1::2]
    starts = [0]
    for kr in k_refs:
        starts.append(starts[-1] + kr.shape[1])

    def pieces(j):
        lo, hi = j * tk, (j + 1) * tk
        return [(s, max(lo, starts[s]) - starts[s], min(hi, starts[s + 1]) - starts[s])
                for s in range(len(k_refs)) if max(lo, starts[s]) < min(hi, starts[s + 1])]

    def cat(parts, axis):
        return parts[0] if len(parts) == 1 else jnp.concatenate(parts, axis=axis)

    def scores(j):
        kb = cat([k_refs[s][0, a:b, :] for s, a, b in pieces(j)], 0)
        return jnp.dot(kb, qpad, preferred_element_type=F32)

    def values(j, with_ones):
        vb = cat([vt_refs[s][0, :, a:b] for s, a, b in pieces(j)], 1)
        return jnp.concatenate([vb, ones_rows], axis=0) if with_ones else vb

    return scores, values, starts[-1] // tk


def _attn_finish(num, den, o_ref, t):
    o = num / den
    heads_per_blk = V7X_LANES // ATT_HEAD_DIM
    for p in range(ATT_REP // heads_per_blk):
        blk = jnp.concatenate(
            [o[:, (p * heads_per_blk + u) * Q_POS:(p * heads_per_blk + u + 1) * Q_POS]
             for u in range(heads_per_blk)], axis=0)
        o_ref[0, t * Q_POS:(t + 1) * Q_POS, p * V7X_LANES:(p + 1) * V7X_LANES] = blk.T.astype(o_ref.dtype)


def _attn_bounded_kernel(qt_ref, *refs, tk):
    *kv_refs, o_ref = refs
    tq = ATT_REP * Q_POS
    for t in range(qt_ref.shape[3] // tq):
        scores, values, nkb = _attn_operands(qt_ref, kv_refs, tk, t)

        def probs(j):
            p = jnp.exp2(scores(j))
            return p.astype(BF16), jnp.sum(p.reshape(tk // V7X_SUBLANES, V7X_SUBLANES, tq), axis=0)

        acc = den = None
        ahead = [probs(j) for j in range(min(ATTN_LOOKAHEAD, nkb))]
        for j in range(nkb):
            if j + ATTN_LOOKAHEAD < nkb:
                ahead.append(probs(j + ATTN_LOOKAHEAD))
            p, psum = ahead.pop(0)
            part = jnp.dot(values(j, False), p, preferred_element_type=F32)
            acc = part if acc is None else acc + part
            den = psum if den is None else den + psum
        _attn_finish(acc, jnp.sum(den, axis=0, keepdims=True), o_ref, t)


def _attn_online_kernel(qt_ref, *refs, tk):
    *kv_refs, o_ref = refs
    tq = ATT_REP * Q_POS
    for t in range(qt_ref.shape[3] // tq):
        scores, values, nkb = _attn_operands(qt_ref, kv_refs, tk, t)
        m = jnp.full((1, tq), NEG_BIG, F32)
        acc = jnp.zeros((SOFTMAX_ROWS, tq), F32)
        for j in range(nkb):
            st = scores(j)
            m_new = jnp.maximum(m, jnp.max(st, axis=0, keepdims=True))
            p = jnp.exp2(st - m_new).astype(BF16)
            acc = jnp.exp2(m - m_new) * acc + jnp.dot(values(j, True), p, preferred_element_type=F32)
            m = m_new
        _attn_finish(acc[:ATT_HEAD_DIM], acc[ATT_HEAD_DIM:ATT_HEAD_DIM + 1], o_ref, t)


def _attn_call(body, name, qt, *kv):
    bsz, g, hd, nq = qt.shape
    nk = sum(k.shape[1] for k in kv[0::2])
    n = nq // ATT_REP
    tiles = max(t for t in range(1, ATTN_TILES + 1) if n % (t * Q_POS) == 0)
    tq = tiles * ATT_REP * Q_POS
    tk = _key_block(nk)
    kern = functools.partial(body, tk=tk)
    est = 4 * nk * V7X_LANES * 2 + 4 * hd * nk * 2 + 10 * tk * ATT_REP * Q_POS * 4
    kv_specs = []
    for k, vt in zip(kv[0::2], kv[1::2]):
        kv_specs.append(pl.BlockSpec((1, k.shape[1], ATT_KV_WIDTH), lambda b, gg, i: (b, 0, 0)))
        kv_specs.append(pl.BlockSpec((1, hd, vt.shape[2]), lambda b, gg, i: (b, gg, 0)))
    return pl.pallas_call(
        kern,
        grid=(bsz, g, nq // tq),
        in_specs=[pl.BlockSpec((1, 1, hd, tq), lambda b, gg, i: (b, gg, 0, i))] + kv_specs,
        out_specs=pl.BlockSpec((1, tiles * Q_POS, ATT_REP * hd), lambda b, gg, i: (b, i, gg)),
        out_shape=jax.ShapeDtypeStruct((bsz, n, ATT_WIDTH), BF16),
        compiler_params=pltpu.CompilerParams(
            dimension_semantics=("arbitrary", "arbitrary", "arbitrary"),
            vmem_limit_bytes=_vmem_limit(est)),
        name=name,
    )(qt, *kv)


def _key_block(nk):
    for tk in (256, 128):
        if nk % tk == 0:
            return tk
    raise ValueError(f"unsupported key count {nk}")


def _attention(qt, kv, bounded):
    return lax.cond(
        bounded,
        functools.partial(_attn_call, _attn_bounded_kernel, "flash_attn_bounded"),
        functools.partial(_attn_call, _attn_online_kernel, "flash_attn_online"),
        qt, *kv)


def _outproj_kernel(x_ref, a_ref, att_ref, cm_ref, mod_ref, w_ref, o_ref):
    mix = (jnp.dot(a_ref[0], w_ref[0, :A_WIDTH, :], preferred_element_type=F32)
           + jnp.dot(att_ref[0], w_ref[0, A_WIDTH:A_WIDTH + ATT_WIDTH, :], preferred_element_type=F32)
           + jnp.dot(cm_ref[0], w_ref[0, A_WIDTH + ATT_WIDTH:, :], preferred_element_type=F32))
    o_ref[0] = x_ref[0] + mod_ref[0, 0][2:3, :] * mix


def _outproj_call(x, a, att, cm, l, mod, mod_row, w_out, tm):
    bsz, n, d = x.shape
    row = lambda w: pl.BlockSpec((1, tm, w), lambda b, i: (b, i, 0))
    est = 6 * tm * d * 4 + d * d * 2 + 4 * tm * d * 2
    return pl.pallas_call(
        _outproj_kernel,
        grid=(bsz, n // tm),
        in_specs=[row(d), row(A_WIDTH), row(ATT_WIDTH), row(C_WIDTH),
                  _mod_spec(mod, l, mod_row),
                  _layer_spec(w_out, l)],
        out_specs=row(d),
        out_shape=jax.ShapeDtypeStruct((bsz, n, d), F32),
        compiler_params=pltpu.CompilerParams(
            dimension_semantics=("arbitrary", "arbitrary"),
            vmem_limit_bytes=_vmem_limit(est)),
        name="outproj",
    )(x, a, att, cm, mod, w_out)


def _ffn_kernel(x_ref, xp_ref, xn_ref, mod_ref, g_ref, wu_ref, wc_ref, wd_ref, fg_ref,
                o_ref, *, tm, nchunk, cf, final_norm):
    mod = mod_ref[0, 0]
    _, h = _halo_norm_modulate(x_ref, xp_ref, xn_ref, g_ref[0], mod[3:4, :], mod[4:5, :])
    up_all = jnp.dot(h, wu_ref[0], preferred_element_type=F32)
    acts = []
    for j in range(nchunk):
        cv = _conv3_seq(up_all[:, j * 2 * cf:(j + 1) * 2 * cf], wc_ref[0, :, j * 2 * cf:(j + 1) * 2 * cf], tm)
        a_ = cv[:, :cf]
        g_ = cv[:, cf:]
        acts.append(((g_ * _sigmoid(g_)) * a_).astype(BF16))
    act = jnp.concatenate(acts, axis=1)
    y = x_ref[0] + mod[5:6, :] * jnp.dot(act, wd_ref[0], preferred_element_type=F32)
    if final_norm:
        ms = jnp.mean(y * y, axis=-1, keepdims=True)
        y = y * lax.rsqrt(ms + EPS) * fg_ref[...]
    o_ref[0] = y


def _ffn_call(x, l, mod, mod_row, g2, wu, wc, wd, fg, tm, final_norm):
    bsz, n, d = x.shape
    d_ff = wd.shape[1]
    cf = FFN_CHUNK
    nchunk = d_ff // cf
    kern = functools.partial(_ffn_kernel, tm=tm, nchunk=nchunk, cf=cf, final_norm=final_norm)
    est = 3 * d * d_ff * 2 + 10 * tm * d * 4 + (tm + 2 * HALO) * 2 * d_ff * 4 + 2 * tm * d_ff * 2
    return pl.pallas_call(
        kern,
        grid=(bsz, n // tm),
        in_specs=_row_tile_specs(tm, n, d) + [
            _mod_spec(mod, l, mod_row),
            _layer_spec(g2, l),
            _layer_spec(wu, l),
            _layer_spec(wc, l),
            _layer_spec(wd, l),
            _const_spec((1, d)),
        ],
        out_specs=pl.BlockSpec((1, tm, d), lambda b, i: (b, i, 0)),
        out_shape=jax.ShapeDtypeStruct((bsz, n, d), F32),
        compiler_params=pltpu.CompilerParams(
            dimension_semantics=("arbitrary", "arbitrary"),
            vmem_limit_bytes=_vmem_limit(est)),
        name="convffn",
    )(x, x, x, mod, g2, wu, wc, wd, fg)


def _rope_tables(n):
    rows = n // GRID_W
    row = jnp.repeat(jnp.arange(rows), GRID_W).astype(F32)
    col = jnp.tile(jnp.arange(GRID_W), rows).astype(F32)
    inv = ROPE_THETA ** (-2.0 * jnp.arange(ROPE_AXIS_FREQS, dtype=F32) / (2 * ROPE_AXIS_FREQS))
    ar = row[:, None] * inv
    ac = col[:, None] * inv
    cos_h = jnp.concatenate([jnp.cos(ar), jnp.cos(ar), jnp.cos(ac), jnp.cos(ac)], axis=1)
    sin_h = jnp.concatenate([-jnp.sin(ar), jnp.sin(ar), -jnp.sin(ac), jnp.sin(ac)], axis=1)
    return jnp.tile(cos_h, (1, 2)), jnp.tile(sin_h, (1, 2))


def _block_ones(width, block):
    idx = jnp.arange(width) // block
    return (idx[:, None] == idx[None, :]).astype(BF16)


def _interleave_gate(w, d_ff, cf):
    lead = w.shape[:-1]
    a = w[..., :d_ff].reshape(lead + (d_ff // cf, cf))
    g = w[..., d_ff:].reshape(lead + (d_ff // cf, cf))
    return jnp.concatenate([a, g], axis=-1).reshape(lead + (2 * d_ff,))


def kernel(x, c, ctx, c_ctx, w_mod, b_mod, norm1_g, w_in, q_norm_g, k_norm_g, gmlp_w, gmlp_b,
           conv_c_w, w_out, norm2_g, ffn_up, ffn_conv_w, ffn_down, final_g):
    bsz, n, d = x.shape
    n_ctx = ctx.shape[1]
    depth = w_mod.shape[0]
    d_ff = ffn_down.shape[1]
    assert n % 512 == 0 and n_ctx % CHUNK == 0 and bsz + 1 <= V7X_SUBLANES

    tm_lat, tm_ctx = 512, n_ctx

    c_rows = jnp.concatenate(
        [c, c_ctx[None, :], jnp.zeros((V7X_SUBLANES - bsz - 1, d), F32)], axis=0)
    mod = _mod_call(c_rows, w_mod, b_mod).reshape(depth, V7X_SUBLANES, N_MOD, d)
    ctx_row = bsz

    cos_l, sin_l = _rope_tables(n)
    cos_c = jnp.ones((n_ctx, V7X_LANES), F32)
    sin_c = jnp.zeros((n_ctx, V7X_LANES), F32)
    jmat = _block_ones(ATT_WIDTH, ATT_HEAD_DIM)
    final_row = final_g.reshape(1, d)

    w_in_b = _cast_stack(w_in, 512)
    w_out_b = _cast_stack(w_out, 512)
    wd_b = _cast_stack(ffn_down, d_ff // 4)
    wu_b = _cast_gate_stack(ffn_up, FFN_CHUNK, 256)
    wc_i = _interleave_gate(ffn_conv_w, d_ff, FFN_CHUNK)
    g1 = norm1_g.reshape(depth, 1, d)
    g2 = norm2_g.reshape(depth, 1, d)
    qg = jnp.tile(q_norm_g, (1, ATT_Q_HEADS)).reshape(depth, 1, ATT_WIDTH)
    kg = jnp.tile(k_norm_g, (1, ATT_KV_HEADS)).reshape(depth, 1, ATT_KV_WIDTH)
    gw = gmlp_w.transpose(0, 2, 1, 3).reshape(depth, CHUNK, A_HEADS * CHUNK).astype(BF16)
    gb = jnp.repeat(gmlp_b.transpose(0, 2, 1), A_HEAD_DIM, axis=2)
    score_bound = (1.02 * ATT_HEAD_DIM * ATT_SCALE * LOG2_E
                   * jnp.max(jnp.abs(q_norm_g), axis=1) * jnp.max(jnp.abs(k_norm_g), axis=1))
    bounded = score_bound <= SCORE_BOUND_LOG2
    inproj_params = (g1, w_in_b, qg, kg)
    inproj_tail = (jmat, gw, gb, conv_c_w)
    ffn_params = (g2, wu_b, wc_i, wd_b, final_row)

    xc = ctx
    for l in range(depth):
        last = l == depth - 1
        q_c, k_c, v_c, a_c, cm_c = _inproj_call(
            xc, l, mod, ctx_row, *inproj_params, cos_c, sin_c, *inproj_tail, tm=tm_ctx)
        if not last:
            att_c = _attention(q_c, (k_c, v_c), bounded[l])
            xc = _outproj_call(xc, a_c, att_c, cm_c, l, mod, ctx_row, w_out_b, tm_ctx)
            xc = _ffn_call(xc, l, mod, ctx_row, *ffn_params, tm_ctx, False)

        q_l, k_l, v_l, a_l, cm_l = _inproj_call(
            x, l, mod, None, *inproj_params, cos_l, sin_l, *inproj_tail, tm=1024)
        att = _attention(q_l, (k_c, v_c, k_l, v_l), bounded[l])
        x = _outproj_call(x, a_l, att, cm_l, l, mod, None, w_out_b, 1024)
        x = _ffn_call(x, l, mod, None, *ffn_params, tm_lat, last)
    return x
```

```python
import functools

import jax
import jax.numpy as jnp
from jax import lax
from jax.experimental import pallas as pl
from jax.experimental.pallas import tpu as pltpu

F32 = jnp.float32
BF16 = jnp.bfloat16

EPS = 1e-6
ROPE_THETA = 10000.0
GRID_W = 64
CHUNK = 128
N_MOD = 6

A_HEADS = 4
A_HEAD_DIM = 64
A_WIDTH = A_HEADS * A_HEAD_DIM
ATT_Q_HEADS = 8
ATT_KV_HEADS = 2
ATT_HEAD_DIM = 64
ATT_REP = ATT_Q_HEADS // ATT_KV_HEADS
ATT_WIDTH = ATT_Q_HEADS * ATT_HEAD_DIM
ATT_KV_WIDTH = ATT_KV_HEADS * ATT_HEAD_DIM
ATT_SCALE = ATT_HEAD_DIM ** -0.5
LOG2_E = 1.4426950408889634
ROPE_AXIS_FREQS = ATT_HEAD_DIM // 4
C_WIDTH = 256

OFF_AU = 0
OFF_AV = OFF_AU + A_WIDTH
OFF_Q = OFF_AV + A_WIDTH
OFF_K = OFF_Q + ATT_WIDTH
OFF_V = OFF_K + ATT_KV_WIDTH
OFF_CB = OFF_V + ATT_KV_WIDTH
OFF_CC = OFF_CB + C_WIDTH
OFF_CH = OFF_CC + C_WIDTH
D_IN = OFF_CH + C_WIDTH

V7X_SUBLANES = 8
V7X_LANES = 128
V7X_VMEM_BYTES = 64 * 1024 * 1024
HALO = V7X_SUBLANES
BF16_ROWS = 16
Q_POS = V7X_LANES
SOFTMAX_ROWS = ATT_HEAD_DIM + BF16_ROWS
NEG_BIG = -1e30
ATTN_LOOKAHEAD = 3
FFN_CHUNK = 256
FFN_SUBTILES = 2
ATTN_TILES = 4
SCORE_BOUND_LOG2 = 50.0


def _vmem_limit(nbytes):
    return int(min(max(nbytes, 16 * 1024 * 1024), V7X_VMEM_BYTES - 6 * 1024 * 1024))


def _sigmoid(x):
    return 1.0 / (1.0 + jnp.exp(-x))


def _gelu_tanh(x):
    c = 0.7978845608028654
    return x * (0.5 * (1.0 + jnp.tanh(c * (x + 0.044715 * (x * x * x)))))


def _split_dot(x, j):
    hi = x.astype(BF16)
    lo = (x - hi.astype(F32)).astype(BF16)
    return (jnp.dot(hi, j, preferred_element_type=F32)
            + jnp.dot(lo, j, preferred_element_type=F32))


def _norm_modulate(x, g, shift, scale):
    ms = jnp.mean(x * x, axis=-1, keepdims=True)
    return (x * lax.rsqrt(ms + EPS) * g) * (1.0 + scale) + shift


def _conv3_seq(u_all, w3, tm):
    rows = u_all.shape[0]
    dn = pltpu.roll(u_all, 1, 0)[HALO:HALO + tm]
    up = pltpu.roll(u_all, rows - 1, 0)[HALO:HALO + tm]
    return dn * w3[0:1, :] + u_all[HALO:HALO + tm] * w3[1:2, :] + up * w3[2:3, :]


def _halo_norm_modulate(x_ref, xp_ref, xn_ref, g, shift, scale):
    return _halo_norm_modulate_rows(x_ref, xp_ref, xn_ref, g, shift, scale, 0, x_ref.shape[1])


def _halo_norm_modulate_rows(x_ref, xp_ref, xn_ref, g, shift, scale, r0, rows):
    i = pl.program_id(1)
    hm = _norm_modulate(x_ref[0, r0:r0 + rows], g, shift, scale)
    if r0 == 0:
        hp = _norm_modulate(xp_ref[0], g, shift, scale) * (i > 0).astype(F32)
    else:
        hp = _norm_modulate(x_ref[0, r0 - HALO:r0], g, shift, scale)
    if r0 + rows == x_ref.shape[1]:
        hn = _norm_modulate(xn_ref[0], g, shift, scale) * (i < pl.num_programs(1) - 1).astype(F32)
    else:
        hn = _norm_modulate(x_ref[0, r0 + rows:r0 + rows + HALO], g, shift, scale)
    return hm.astype(BF16), jnp.concatenate([hp, hm, hn], axis=0).astype(BF16)


def _rope128(x, cos_t, sin_t, odd16):
    sw = jnp.where(odd16, pltpu.roll(x, 16, 1), pltpu.roll(x, V7X_LANES - 16, 1))
    return x * cos_t + sw * sin_t


def _layer_spec(arr, l):
    nd = arr.ndim
    return pl.BlockSpec((1,) + arr.shape[1:], lambda b, i: (l,) + (0,) * (nd - 1),
                        pipeline_mode=pl.Buffered(1))


def _mod_spec(mod, l, row):
    blk = (1, 1) + mod.shape[2:]
    if row is None:
        return pl.BlockSpec(blk, lambda b, i: (l, b, 0, 0))
    return pl.BlockSpec(blk, lambda b, i: (l, row, 0, 0))


def _cast_kernel(w_ref, o_ref):
    o_ref[...] = w_ref[...].astype(o_ref.dtype)


def _cast_stack(w, rows):
    depth, r, c = w.shape
    spec = pl.BlockSpec((1, rows, c), lambda l, i: (l, i, 0))
    return pl.pallas_call(
        _cast_kernel,
        grid=(depth, r // rows),
        in_specs=[spec],
        out_specs=spec,
        out_shape=jax.ShapeDtypeStruct(w.shape, BF16),
        compiler_params=pltpu.CompilerParams(
            dimension_semantics=("arbitrary", "arbitrary"),
            vmem_limit_bytes=_vmem_limit(6 * rows * c * 4)),
        name="cast_bf16",
    )(w)


def _cast_gate_kernel(a_ref, g_ref, o_ref, *, cf):
    for j in range(a_ref.shape[2] // cf):
        o_ref[0, :, 2 * j * cf:(2 * j + 1) * cf] = a_ref[0, :, j * cf:(j + 1) * cf].astype(o_ref.dtype)
        o_ref[0, :, (2 * j + 1) * cf:(2 * j + 2) * cf] = g_ref[0, :, j * cf:(j + 1) * cf].astype(o_ref.dtype)


def _cast_gate_stack(w, cf, rows):
    depth, r, c2 = w.shape
    d_ff = c2 // 2
    return pl.pallas_call(
        functools.partial(_cast_gate_kernel, cf=cf),
        grid=(depth, r // rows),
        in_specs=[pl.BlockSpec((1, rows, d_ff), lambda l, i: (l, i, 0)),
                  pl.BlockSpec((1, rows, d_ff), lambda l, i: (l, i, 1))],
        out_specs=pl.BlockSpec((1, rows, c2), lambda l, i: (l, i, 0)),
        out_shape=jax.ShapeDtypeStruct(w.shape, BF16),
        compiler_params=pltpu.CompilerParams(
            dimension_semantics=("arbitrary", "arbitrary"),
            vmem_limit_bytes=_vmem_limit(8 * rows * c2 * 4)),
        name="cast_gate_bf16",
    )(w, w)


def _mod_kernel(c_ref, w_ref, b_ref, o_ref):
    c = c_ref[...]
    s = (c * _sigmoid(c)).astype(BF16)
    o_ref[0] = jnp.dot(s, w_ref[0].astype(BF16), preferred_element_type=F32) + b_ref[0]


def _mod_call(c_rows, w_mod, b_mod):
    depth, d, nm = w_mod.shape
    bn = 1536
    return pl.pallas_call(
        _mod_kernel,
        grid=(depth, nm // bn),
        in_specs=[
            pl.BlockSpec((V7X_SUBLANES, d), lambda l, j: (0, 0)),
            pl.BlockSpec((1, d, bn), lambda l, j: (l, 0, j)),
            pl.BlockSpec((1, 1, bn), lambda l, j: (l, 0, j)),
        ],
        out_specs=pl.BlockSpec((1, V7X_SUBLANES, bn), lambda l, j: (l, 0, j)),
        out_shape=jax.ShapeDtypeStruct((depth, V7X_SUBLANES, nm), F32),
        compiler_params=pltpu.CompilerParams(
            dimension_semantics=("arbitrary", "arbitrary"),
            vmem_limit_bytes=_vmem_limit(3 * d * bn * 4)),
        name="adaln_mod",
    )(c_rows, w_mod, b_mod.reshape(depth, 1, nm))


def _inproj_kernel(x_ref, xp_ref, xn_ref, mod_ref, g_ref, w_ref, qg_ref, kg_ref,
                   cos_ref, sin_ref, j_ref, gw_ref, gb_ref, cw_ref,
                   q_ref, k_ref, v_ref, a_ref, cm_ref, *, tm):
    mod = mod_ref[0, 0]
    h, h_all = _halo_norm_modulate(x_ref, xp_ref, xn_ref, g_ref[0], mod[0:1, :], mod[1:2, :])
    proj = jnp.dot(h, w_ref[0, :, :OFF_CC], preferred_element_type=F32)
    pch = jnp.dot(h_all, w_ref[0, :, OFF_CC:], preferred_element_type=F32)

    cos_t = cos_ref[...]
    sin_t = sin_ref[...]
    lane = lax.broadcasted_iota(jnp.int32, (tm, V7X_LANES), 1)
    odd16 = (lane & 16) != 0
    inv_hd = 1.0 / ATT_HEAD_DIM

    q = proj[:, OFF_Q:OFF_K]
    qn = q * lax.rsqrt(_split_dot(q * q, j_ref[...]) * inv_hd + EPS) * qg_ref[0]
    heads_per_blk = V7X_LANES // ATT_HEAD_DIM
    for t in range(ATT_WIDTH // V7X_LANES):
        blk = _rope128(qn[:, t * V7X_LANES:(t + 1) * V7X_LANES], cos_t, sin_t, odd16)
        blk_t = (blk * (ATT_SCALE * LOG2_E)).T.astype(BF16)
        for u in range(heads_per_blk):
            gi, ri = divmod(t * heads_per_blk + u, ATT_REP)
            for s in range(tm // Q_POS):
                col = s * ATT_REP * Q_POS + ri * Q_POS
                q_ref[0, gi, :, col:col + Q_POS] = blk_t[u * ATT_HEAD_DIM:(u + 1) * ATT_HEAD_DIM,
                                                         s * Q_POS:(s + 1) * Q_POS]
    k = proj[:, OFF_K:OFF_V]
    kn = k * lax.rsqrt(_split_dot(k * k, j_ref[:ATT_KV_WIDTH, :ATT_KV_WIDTH]) * inv_hd + EPS) * kg_ref[0]
    k_ref[0] = _rope128(kn, cos_t, sin_t, odd16).astype(BF16)
    v_ref[0] = proj[:, OFF_V:OFF_CB].T.astype(BF16)

    u_act = _gelu_tanh(proj[:, OFF_AU:OFF_AV])
    v_act = _gelu_tanh(proj[:, OFF_AV:OFF_Q])
    jl = j_ref[:A_WIDTH, :A_WIDTH]
    mu = _split_dot(v_act, jl) * (1.0 / A_HEAD_DIM)
    xc = v_act - mu
    var = _split_dot(xc * xc, jl) * (1.0 / A_HEAD_DIM)
    vln = xc * lax.rsqrt(var + EPS)
    lane_a = lax.broadcasted_iota(jnp.int32, (CHUNK, A_WIDTH), 1)
    for c in range(tm // CHUNK):
        vc = vln[c * CHUNK:(c + 1) * CHUNK]
        vs = jnp.concatenate(
            [jnp.where((lane_a // A_HEAD_DIM) == hh, vc, 0.0).astype(BF16) for hh in range(A_HEADS)],
            axis=0)
        mixed = jnp.dot(gw_ref[0], vs, preferred_element_type=F32) + gb_ref[0]
        a_ref[0, c * CHUNK:(c + 1) * CHUNK, :] = (u_act[c * CHUNK:(c + 1) * CHUNK] * mixed).astype(BF16)

    ch = pch[:, :C_WIDTH] * pch[:, C_WIDTH:]
    cm = proj[:, OFF_CB:OFF_CC] * _conv3_seq(ch, cw_ref[0], tm)
    cm_ref[0] = cm.astype(BF16)


def _row_tile_specs(tm, n, d):
    nb8 = n // HALO
    r = tm // HALO
    return [
        pl.BlockSpec((1, tm, d), lambda b, i: (b, i, 0)),
        pl.BlockSpec((1, HALO, d), lambda b, i: (b, jnp.maximum(i * r - 1, 0), 0)),
        pl.BlockSpec((1, HALO, d), lambda b, i: (b, jnp.minimum((i + 1) * r, nb8 - 1), 0)),
    ]


def _const_spec(shape):
    nd = len(shape)
    return pl.BlockSpec(shape, lambda b, i: (0,) * nd, pipeline_mode=pl.Buffered(1))


def _inproj_call(x, l, mod, mod_row, g1, w_in, qg, kg, cos_t, sin_t, jmat, gw, gb, cw, tm):
    bsz, n, d = x.shape
    kern = functools.partial(_inproj_kernel, tm=tm)
    row = lambda w: pl.BlockSpec((1, tm, w), lambda b, i: (b, i, 0))
    est = (4 * tm * d * 4 + d * D_IN * 2 + 8 * tm * D_IN * 4 + 4 * tm * 2048 * 2)
    return pl.pallas_call(
        kern,
        grid=(bsz, n // tm),
        in_specs=_row_tile_specs(tm, n, d) + [
            _mod_spec(mod, l, mod_row),
            _layer_spec(g1, l),
            _layer_spec(w_in, l),
            _layer_spec(qg, l),
            _layer_spec(kg, l),
            pl.BlockSpec((tm, V7X_LANES), lambda b, i: (i, 0)),
            pl.BlockSpec((tm, V7X_LANES), lambda b, i: (i, 0)),
            _const_spec(jmat.shape),
            _layer_spec(gw, l),
            _layer_spec(gb, l),
            _layer_spec(cw, l),
        ],
        out_specs=[
            pl.BlockSpec((1, ATT_KV_HEADS, ATT_HEAD_DIM, ATT_REP * tm), lambda b, i: (b, 0, 0, i)),
            row(ATT_KV_WIDTH),
            pl.BlockSpec((1, ATT_KV_WIDTH, tm), lambda b, i: (b, 0, i)),
            row(A_WIDTH), row(C_WIDTH)],
        out_shape=[
            jax.ShapeDtypeStruct((bsz, ATT_KV_HEADS, ATT_HEAD_DIM, ATT_REP * n), BF16),
            jax.ShapeDtypeStruct((bsz, n, ATT_KV_WIDTH), BF16),
            jax.ShapeDtypeStruct((bsz, ATT_KV_WIDTH, n), BF16),
            jax.ShapeDtypeStruct((bsz, n, A_WIDTH), BF16),
            jax.ShapeDtypeStruct((bsz, n, C_WIDTH), BF16),
        ],
        compiler_params=pltpu.CompilerParams(
            dimension_semantics=("arbitrary", "arbitrary"),
            vmem_limit_bytes=_vmem_limit(est)),
        name="inproj",
    )(x, x, x, mod, g1, w_in, qg, kg, cos_t, sin_t, jmat, gw, gb, cw)


def _attn_operands(qt_ref, kv_refs, tk, t):
    g = pl.program_id(1)
    tq = ATT_REP * Q_POS
    qt = qt_ref[0, 0, :, t * tq:(t + 1) * tq]
    zero = jnp.zeros_like(qt)
    qpad = jnp.concatenate([jnp.where(g == gg, qt, zero) for gg in range(ATT_KV_HEADS)], axis=0)
    ones_rows = (lax.broadcasted_iota(jnp.int32, (BF16_ROWS, tk), 0) == 0).astype(BF16)
    k_refs, vt_refs = kv_refs[0::2], kv_refs[1::2]
    starts = [0]
    for kr in k_refs:
        starts.append(starts[-1] + kr.shape[1])

    def pieces(j):
        lo, hi = j * tk, (j + 1) * tk
        return [(s, max(lo, starts[s]) - starts[s], min(hi, starts[s + 1]) - starts[s])
                for s in range(len(k_refs)) if max(lo, starts[s]) < min(hi, starts[s + 1])]

    def cat(parts, axis):
        return parts[0] if len(parts) == 1 else jnp.concatenate(parts, axis=axis)

    def scores(j):
        kb = cat([k_refs[s][0, a:b, :] for s, a, b in pieces(j)], 0)
        return jnp.dot(kb, qpad, preferred_element_type=F32)

    def values(j, with_ones):
        vb = cat([vt_refs[s][0, :, a:b] for s, a, b in pieces(j)], 1)
        return jnp.concatenate([vb, ones_rows], axis=0) if with_ones else vb

    return scores, values, starts[-1] // tk


def _attn_finish(num, den, o_ref, t):
    o = num / den
    heads_per_blk = V7X_LANES // ATT_HEAD_DIM
    for p in range(ATT_REP // heads_per_blk):
        blk = jnp.concatenate(
            [o[:, (p * heads_per_blk + u) * Q_POS:(p * heads_per_blk + u + 1) * Q_POS]
             for u in range(heads_per_blk)], axis=0)
        o_ref[0, t * Q_POS:(t + 1) * Q_POS, p * V7X_LANES:(p + 1) * V7X_LANES] = blk.T.astype(o_ref.dtype)


def _attn_bounded_kernel(qt_ref, *refs, tk):
    *kv_refs, o_ref = refs
    tq = ATT_REP * Q_POS
    for t in range(qt_ref.shape[3] // tq):
        scores, values, nkb = _attn_operands(qt_ref, kv_refs, tk, t)

        def probs(j):
            p = jnp.exp2(scores(j))
            return p.astype(BF16), jnp.sum(p.reshape(tk // V7X_SUBLANES, V7X_SUBLANES, tq), axis=0)

        acc = den = None
        ahead = [probs(j) for j in range(min(ATTN_LOOKAHEAD, nkb))]
        for j in range(nkb):
            if j + ATTN_LOOKAHEAD < nkb:
                ahead.append(probs(j + ATTN_LOOKAHEAD))
            p, psum = ahead.pop(0)
            part = jnp.dot(values(j, False), p, preferred_element_type=F32)
            acc = part if acc is None else acc + part
            den = psum if den is None else den + psum
        _attn_finish(acc, jnp.sum(den, axis=0, keepdims=True), o_ref, t)


def _attn_online_kernel(qt_ref, *refs, tk):
    *kv_refs, o_ref = refs
    tq = ATT_REP * Q_POS
    for t in range(qt_ref.shape[3] // tq):
        scores, values, nkb = _attn_operands(qt_ref, kv_refs, tk, t)
        m = jnp.full((1, tq), NEG_BIG, F32)
        acc = jnp.zeros((SOFTMAX_ROWS, tq), F32)
        for j in range(nkb):
            st = scores(j)
            m_new = jnp.maximum(m, jnp.max(st, axis=0, keepdims=True))
            p = jnp.exp2(st - m_new).astype(BF16)
            acc = jnp.exp2(m - m_new) * acc + jnp.dot(values(j, True), p, preferred_element_type=F32)
            m = m_new
        _attn_finish(acc[:ATT_HEAD_DIM], acc[ATT_HEAD_DIM:ATT_HEAD_DIM + 1], o_ref, t)


def _attn_call(body, name, qt, *kv):
    bsz, g, hd, nq = qt.shape
    nk = sum(k.shape[1] for k in kv[0::2])
    n = nq // ATT_REP
    tiles = max(t for t in range(1, ATTN_TILES + 1) if n % (t * Q_POS) == 0)
    tq = tiles * ATT_REP * Q_POS
    tk = _key_block(nk)
    kern = functools.partial(body, tk=tk)
    est = 4 * nk * V7X_LANES * 2 + 4 * hd * nk * 2 + 10 * tk * ATT_REP * Q_POS * 4
    kv_specs = []
    for k, vt in zip(kv[0::2], kv[1::2]):
        kv_specs.append(pl.BlockSpec((1, k.shape[1], ATT_KV_WIDTH), lambda b, gg, i: (b, 0, 0)))
        kv_specs.append(pl.BlockSpec((1, hd, vt.shape[2]), lambda b, gg, i: (b, gg, 0)))
    return pl.pallas_call(
        kern,
        grid=(bsz, g, nq // tq),
        in_specs=[pl.BlockSpec((1, 1, hd, tq), lambda b, gg, i: (b, gg, 0, i))] + kv_specs,
        out_specs=pl.BlockSpec((1, tiles * Q_POS, ATT_REP * hd), lambda b, gg, i: (b, i, gg)),
        out_shape=jax.ShapeDtypeStruct((bsz, n, ATT_WIDTH), BF16),
        compiler_params=pltpu.CompilerParams(
            dimension_semantics=("arbitrary", "arbitrary", "arbitrary"),
            vmem_limit_bytes=_vmem_limit(est)),
        name=name,
    )(qt, *kv)


def _key_block(nk):
    for tk in (256, 128):
        if nk % tk == 0:
            return tk
    raise ValueError(f"unsupported key count {nk}")


def _attention(qt, kv, bounded):
    return lax.cond(
        bounded,
        functools.partial(_attn_call, _attn_bounded_kernel, "flash_attn_bounded"),
        functools.partial(_attn_call, _attn_online_kernel, "flash_attn_online"),
        qt, *kv)


def _outproj_kernel(x_ref, a_ref, att_ref, cm_ref, mod_ref, w_ref, o_ref):
    mix = (jnp.dot(a_ref[0], w_ref[0, :A_WIDTH, :], preferred_element_type=F32)
           + jnp.dot(att_ref[0], w_ref[0, A_WIDTH:A_WIDTH + ATT_WIDTH, :], preferred_element_type=F32)
           + jnp.dot(cm_ref[0], w_ref[0, A_WIDTH + ATT_WIDTH:, :], preferred_element_type=F32))
    o_ref[0] = x_ref[0] + mod_ref[0, 0][2:3, :] * mix


def _outproj_call(x, a, att, cm, l, mod, mod_row, w_out, tm):
    bsz, n, d = x.shape
    row = lambda w: pl.BlockSpec((1, tm, w), lambda b, i: (b, i, 0))
    est = 6 * tm * d * 4 + d * d * 2 + 4 * tm * d * 2
    return pl.pallas_call(
        _outproj_kernel,
        grid=(bsz, n // tm),
        in_specs=[row(d), row(A_WIDTH), row(ATT_WIDTH), row(C_WIDTH),
                  _mod_spec(mod, l, mod_row),
                  _layer_spec(w_out, l)],
        out_specs=row(d),
        out_shape=jax.ShapeDtypeStruct((bsz, n, d), F32),
        compiler_params=pltpu.CompilerParams(
            dimension_semantics=("arbitrary", "arbitrary"),
            vmem_limit_bytes=_vmem_limit(est)),
        name="outproj",
    )(x, a, att, cm, mod, w_out)


def _ffn_kernel(x_ref, xp_ref, xn_ref, mod_ref, g_ref, wu_ref, wc_ref, wd_ref, fg_ref,
                o_ref, *, tm, nchunk, cf, final_norm):
    mod = mod_ref[0, 0]
    for r0 in range(0, x_ref.shape[1], tm):
        _, h = _halo_norm_modulate_rows(x_ref, xp_ref, xn_ref, g_ref[0], mod[3:4, :], mod[4:5, :], r0, tm)
        up_all = jnp.dot(h, wu_ref[0], preferred_element_type=F32)
        acts = []
        for j in range(nchunk):
            cv = _conv3_seq(up_all[:, j * 2 * cf:(j + 1) * 2 * cf], wc_ref[0, :, j * 2 * cf:(j + 1) * 2 * cf], tm)
            a_ = cv[:, :cf]
            g_ = cv[:, cf:]
            acts.append(((g_ * _sigmoid(g_)) * a_).astype(BF16))
        act = jnp.concatenate(acts, axis=1)
        y = x_ref[0, r0:r0 + tm] + mod[5:6, :] * jnp.dot(act, wd_ref[0], preferred_element_type=F32)
        if final_norm:
            ms = jnp.mean(y * y, axis=-1, keepdims=True)
            y = y * lax.rsqrt(ms + EPS) * fg_ref[...]
        o_ref[0, r0:r0 + tm] = y


def _ffn_call(x, l, mod, mod_row, g2, wu, wc, wd, fg, tm, final_norm):
    bsz, n, d = x.shape
    d_ff = wd.shape[1]
    cf = FFN_CHUNK
    nchunk = d_ff // cf
    kern = functools.partial(_ffn_kernel, tm=tm, nchunk=nchunk, cf=cf, final_norm=final_norm)
    tb = FFN_SUBTILES * tm if n % (FFN_SUBTILES * tm) == 0 else tm
    est = 3 * d * d_ff * 2 + 6 * tb * d * 4 + 4 * tm * d * 4 + (tm + 2 * HALO) * 2 * d_ff * 4 + 4 * tm * d_ff * 2
    return pl.pallas_call(
        kern,
        grid=(bsz, n // tb),
        in_specs=_row_tile_specs(tb, n, d) + [
            _mod_spec(mod, l, mod_row),
            _layer_spec(g2, l),
            _layer_spec(wu, l),
            _layer_spec(wc, l),
            _layer_spec(wd, l),
            _const_spec((1, d)),
        ],
        out_specs=pl.BlockSpec((1, tb, d), lambda b, i: (b, i, 0)),
        out_shape=jax.ShapeDtypeStruct((bsz, n, d), F32),
        compiler_params=pltpu.CompilerParams(
            dimension_semantics=("arbitrary", "arbitrary"),
            vmem_limit_bytes=_vmem_limit(est)),
        name="convffn",
    )(x, x, x, mod, g2, wu, wc, wd, fg)


def _rope_tables(n):
    rows = n // GRID_W
    row = jnp.repeat(jnp.arange(rows), GRID_W).astype(F32)
    col = jnp.tile(jnp.arange(GRID_W), rows).astype(F32)
    inv = ROPE_THETA ** (-2.0 * jnp.arange(ROPE_AXIS_FREQS, dtype=F32) / (2 * ROPE_AXIS_FREQS))
    ar = row[:, None] * inv
    ac = col[:, None] * inv
    cos_h = jnp.concatenate([jnp.cos(ar), jnp.cos(ar), jnp.cos(ac), jnp.cos(ac)], axis=1)
    sin_h = jnp.concatenate([-jnp.sin(ar), jnp.sin(ar), -jnp.sin(ac), jnp.sin(ac)], axis=1)
    return jnp.tile(cos_h, (1, 2)), jnp.tile(sin_h, (1, 2))


def _block_ones(width, block):
    idx = jnp.arange(width) // block
    return (idx[:, None] == idx[None, :]).astype(BF16)


def _interleave_gate(w, d_ff, cf):
    lead = w.shape[:-1]
    a = w[..., :d_ff].reshape(lead + (d_ff // cf, cf))
    g = w[..., d_ff:].reshape(lead + (d_ff // cf, cf))
    return jnp.concatenate([a, g], axis=-1).reshape(lead + (2 * d_ff,))


def kernel(x, c, ctx, c_ctx, w_mod, b_mod, norm1_g, w_in, q_norm_g, k_norm_g, gmlp_w, gmlp_b,
           conv_c_w, w_out, norm2_g, ffn_up, ffn_conv_w, ffn_down, final_g):
    bsz, n, d = x.shape
    n_ctx = ctx.shape[1]
    depth = w_mod.shape[0]
    d_ff = ffn_down.shape[1]
    tm_proj, tm_lat, tm_ctx = 1024, 512, n_ctx
    assert n % tm_proj == 0 and n % GRID_W == 0 and n_ctx % CHUNK == 0 and bsz + 1 <= V7X_SUBLANES

    c_rows = jnp.concatenate(
        [c, c_ctx[None, :], jnp.zeros((V7X_SUBLANES - bsz - 1, d), F32)], axis=0)
    mod = _mod_call(c_rows, w_mod, b_mod).reshape(depth, V7X_SUBLANES, N_MOD, d)
    ctx_row = bsz

    cos_l, sin_l = _rope_tables(n)
    cos_c = jnp.ones((n_ctx, V7X_LANES), F32)
    sin_c = jnp.zeros((n_ctx, V7X_LANES), F32)
    jmat = _block_ones(ATT_WIDTH, ATT_HEAD_DIM)
    final_row = final_g.reshape(1, d)

    w_in_b = _cast_stack(w_in, 512)
    w_out_b = _cast_stack(w_out, 512)
    wd_b = _cast_stack(ffn_down, d_ff // 4)
    wu_b = _cast_gate_stack(ffn_up, FFN_CHUNK, 256)
    wc_i = _interleave_gate(ffn_conv_w, d_ff, FFN_CHUNK)
    g1 = norm1_g.reshape(depth, 1, d)
    g2 = norm2_g.reshape(depth, 1, d)
    qg = jnp.tile(q_norm_g, (1, ATT_Q_HEADS)).reshape(depth, 1, ATT_WIDTH)
    kg = jnp.tile(k_norm_g, (1, ATT_KV_HEADS)).reshape(depth, 1, ATT_KV_WIDTH)
    gw = gmlp_w.transpose(0, 2, 1, 3).reshape(depth, CHUNK, A_HEADS * CHUNK).astype(BF16)
    gb = jnp.repeat(gmlp_b.transpose(0, 2, 1), A_HEAD_DIM, axis=2)
    score_bound = (1.02 * ATT_HEAD_DIM * ATT_SCALE * LOG2_E
                   * jnp.max(jnp.abs(q_norm_g), axis=1) * jnp.max(jnp.abs(k_norm_g), axis=1))
    bounded = score_bound <= SCORE_BOUND_LOG2
    inproj_params = (g1, w_in_b, qg, kg)
    inproj_tail = (jmat, gw, gb, conv_c_w)
    ffn_params = (g2, wu_b, wc_i, wd_b, final_row)

    xc = ctx
    for l in range(depth):
        last = l == depth - 1
        q_c, k_c, v_c, a_c, cm_c = _inproj_call(
            xc, l, mod, ctx_row, *inproj_params, cos_c, sin_c, *inproj_tail, tm=tm_ctx)
        if not last:
            att_c = _attention(q_c, (k_c, v_c), bounded[l])
            xc = _outproj_call(xc, a_c, att_c, cm_c, l, mod, ctx_row, w_out_b, tm_ctx)
            xc = _ffn_call(xc, l, mod, ctx_row, *ffn_params, tm_ctx, False)

        q_l, k_l, v_l, a_l, cm_l = _inproj_call(
            x, l, mod, None, *inproj_params, cos_l, sin_l, *inproj_tail, tm=tm_proj)
        att = _attention(q_l, (k_c, v_c, k_l, v_l), bounded[l])
        x = _outproj_call(x, a_l, att, cm_l, l, mod, None, w_out_b, tm_proj)
        x = _ffn_call(x, l, mod, None, *ffn_params, tm_lat, last)
    return x
```

```python
import functools

import jax
import jax.numpy as jnp
from jax import lax
from jax.experimental import pallas as pl
from jax.experimental.pallas import tpu as pltpu

F32 = jnp.float32
BF16 = jnp.bfloat16

EPS = 1e-6
ROPE_THETA = 10000.0
GRID_W = 64
CHUNK = 128
N_MOD = 6

A_HEADS = 4
A_HEAD_DIM = 64
A_WIDTH = A_HEADS * A_HEAD_DIM
ATT_Q_HEADS = 8
ATT_KV_HEADS = 2
ATT_HEAD_DIM = 64
ATT_REP = ATT_Q_HEADS // ATT_KV_HEADS
ATT_WIDTH = ATT_Q_HEADS * ATT_HEAD_DIM
ATT_KV_WIDTH = ATT_KV_HEADS * ATT_HEAD_DIM
ATT_SCALE = ATT_HEAD_DIM ** -0.5
LOG2_E = 1.4426950408889634
ROPE_AXIS_FREQS = ATT_HEAD_DIM // 4
C_WIDTH = 256

OFF_AU = 0
OFF_AV = OFF_AU + A_WIDTH
OFF_Q = OFF_AV + A_WIDTH
OFF_K = OFF_Q + ATT_WIDTH
OFF_V = OFF_K + ATT_KV_WIDTH
OFF_CB = OFF_V + ATT_KV_WIDTH
OFF_CC = OFF_CB + C_WIDTH
OFF_CH = OFF_CC + C_WIDTH
D_IN = OFF_CH + C_WIDTH

V7X_SUBLANES = 8
V7X_LANES = 128
V7X_VMEM_BYTES = 64 * 1024 * 1024
HALO = V7X_SUBLANES
BF16_ROWS = 16
Q_POS = V7X_LANES
SOFTMAX_ROWS = ATT_HEAD_DIM + BF16_ROWS
NEG_BIG = -1e30
ATTN_LOOKAHEAD = 3
FFN_CHUNK = 256
FFN_SUBTILES = 2
ATTN_TILES = 4
SCORE_BOUND_LOG2 = 50.0


def _vmem_limit(nbytes):
    return int(min(max(nbytes, 16 * 1024 * 1024), V7X_VMEM_BYTES - 6 * 1024 * 1024))


def _sigmoid(x):
    return 1.0 / (1.0 + jnp.exp(-x))


def _gelu_tanh(x):
    c = 0.7978845608028654
    return x * (0.5 * (1.0 + jnp.tanh(c * (x + 0.044715 * (x * x * x)))))


def _split_dot(x, j):
    hi = x.astype(BF16)
    lo = (x - hi.astype(F32)).astype(BF16)
    return (jnp.dot(hi, j, preferred_element_type=F32)
            + jnp.dot(lo, j, preferred_element_type=F32))


def _norm_modulate(x, g, shift, scale):
    ms = jnp.mean(x * x, axis=-1, keepdims=True)
    return (x * lax.rsqrt(ms + EPS) * g) * (1.0 + scale) + shift


def _conv3_seq(u_all, w3, tm):
    rows = u_all.shape[0]
    dn = pltpu.roll(u_all, 1, 0)[HALO:HALO + tm]
    up = pltpu.roll(u_all, rows - 1, 0)[HALO:HALO + tm]
    return dn * w3[0:1, :] + u_all[HALO:HALO + tm] * w3[1:2, :] + up * w3[2:3, :]


def _halo_norm_modulate(x_ref, xp_ref, xn_ref, g, shift, scale):
    return _halo_norm_modulate_rows(x_ref, xp_ref, xn_ref, g, shift, scale, 0, x_ref.shape[1])


def _halo_norm_modulate_rows(x_ref, xp_ref, xn_ref, g, shift, scale, r0, rows):
    i = pl.program_id(1)
    hm = _norm_modulate(x_ref[0, r0:r0 + rows], g, shift, scale)
    if r0 == 0:
        hp = _norm_modulate(xp_ref[0], g, shift, scale) * (i > 0).astype(F32)
    else:
        hp = _norm_modulate(x_ref[0, r0 - HALO:r0], g, shift, scale)
    if r0 + rows == x_ref.shape[1]:
        hn = _norm_modulate(xn_ref[0], g, shift, scale) * (i < pl.num_programs(1) - 1).astype(F32)
    else:
        hn = _norm_modulate(x_ref[0, r0 + rows:r0 + rows + HALO], g, shift, scale)
    return hm.astype(BF16), jnp.concatenate([hp, hm, hn], axis=0).astype(BF16)


def _rope128(x, cos_t, sin_t, odd16):
    sw = jnp.where(odd16, pltpu.roll(x, 16, 1), pltpu.roll(x, V7X_LANES - 16, 1))
    return x * cos_t + sw * sin_t


def _layer_spec(arr, l):
    nd = arr.ndim
    return pl.BlockSpec((1,) + arr.shape[1:], lambda b, i: (l,) + (0,) * (nd - 1),
                        pipeline_mode=pl.Buffered(1))


def _mod_spec(mod, l, row):
    blk = (1, 1) + mod.shape[2:]
    if row is None:
        return pl.BlockSpec(blk, lambda b, i: (l, b, 0, 0))
    return pl.BlockSpec(blk, lambda b, i: (l, row, 0, 0))


def _cast_kernel(w_ref, o_ref):
    o_ref[...] = w_ref[...].astype(o_ref.dtype)


def _cast_stack(w, rows):
    depth, r, c = w.shape
    spec = pl.BlockSpec((1, rows, c), lambda l, i: (l, i, 0))
    return pl.pallas_call(
        _cast_kernel,
        grid=(depth, r // rows),
        in_specs=[spec],
        out_specs=spec,
        out_shape=jax.ShapeDtypeStruct(w.shape, BF16),
        compiler_params=pltpu.CompilerParams(
            dimension_semantics=("arbitrary", "arbitrary"),
            vmem_limit_bytes=_vmem_limit(6 * rows * c * 4)),
        name="cast_bf16",
    )(w)


def _cast_gate_kernel(a_ref, g_ref, o_ref, *, cf):
    for j in range(a_ref.shape[2] // cf):
        o_ref[0, :, 2 * j * cf:(2 * j + 1) * cf] = a_ref[0, :, j * cf:(j + 1) * cf].astype(o_ref.dtype)
        o_ref[0, :, (2 * j + 1) * cf:(2 * j + 2) * cf] = g_ref[0, :, j * cf:(j + 1) * cf].astype(o_ref.dtype)


def _cast_gate_stack(w, cf, rows):
    depth, r, c2 = w.shape
    d_ff = c2 // 2
    return pl.pallas_call(
        functools.partial(_cast_gate_kernel, cf=cf),
        grid=(depth, r // rows),
        in_specs=[pl.BlockSpec((1, rows, d_ff), lambda l, i: (l, i, 0)),
                  pl.BlockSpec((1, rows, d_ff), lambda l, i: (l, i, 1))],
        out_specs=pl.BlockSpec((1, rows, c2), lambda l, i: (l, i, 0)),
        out_shape=jax.ShapeDtypeStruct(w.shape, BF16),
        compiler_params=pltpu.CompilerParams(
            dimension_semantics=("arbitrary", "arbitrary"),
            vmem_limit_bytes=_vmem_limit(8 * rows * c2 * 4)),
        name="cast_gate_bf16",
    )(w, w)


def _mod_kernel(c_ref, w_ref, b_ref, o_ref):
    c = c_ref[...]
    s = (c * _sigmoid(c)).astype(BF16)
    o_ref[0] = jnp.dot(s, w_ref[0].astype(BF16), preferred_element_type=F32) + b_ref[0]


def _mod_call(c_rows, w_mod, b_mod):
    depth, d, nm = w_mod.shape
    bn = 1536
    return pl.pallas_call(
        _mod_kernel,
        grid=(depth, nm // bn),
        in_specs=[
            pl.BlockSpec((V7X_SUBLANES, d), lambda l, j: (0, 0)),
            pl.BlockSpec((1, d, bn), lambda l, j: (l, 0, j)),
            pl.BlockSpec((1, 1, bn), lambda l, j: (l, 0, j)),
        ],
        out_specs=pl.BlockSpec((1, V7X_SUBLANES, bn), lambda l, j: (l, 0, j)),
        out_shape=jax.ShapeDtypeStruct((depth, V7X_SUBLANES, nm), F32),
        compiler_params=pltpu.CompilerParams(
            dimension_semantics=("arbitrary", "arbitrary"),
            vmem_limit_bytes=_vmem_limit(3 * d * bn * 4)),
        name="adaln_mod",
    )(c_rows, w_mod, b_mod.reshape(depth, 1, nm))


def _inproj_kernel(x_ref, xp_ref, xn_ref, mod_ref, g_ref, w_ref, qg_ref, kg_ref,
                   cos_ref, sin_ref, j_ref, gw_ref, gb_ref, cw_ref,
                   q_ref, k_ref, v_ref, a_ref, cm_ref, *, tm):
    mod = mod_ref[0, 0]
    h, h_all = _halo_norm_modulate(x_ref, xp_ref, xn_ref, g_ref[0], mod[0:1, :], mod[1:2, :])
    proj = jnp.dot(h, w_ref[0, :, :OFF_CC], preferred_element_type=F32)
    pch = jnp.dot(h_all, w_ref[0, :, OFF_CC:], preferred_element_type=F32)

    cos_t = cos_ref[...]
    sin_t = sin_ref[...]
    lane = lax.broadcasted_iota(jnp.int32, (tm, V7X_LANES), 1)
    odd16 = (lane & 16) != 0
    inv_hd = 1.0 / ATT_HEAD_DIM

    q = proj[:, OFF_Q:OFF_K]
    qn = q * lax.rsqrt(_split_dot(q * q, j_ref[...]) * inv_hd + EPS) * qg_ref[0]
    heads_per_blk = V7X_LANES // ATT_HEAD_DIM
    for t in range(ATT_WIDTH // V7X_LANES):
        blk = _rope128(qn[:, t * V7X_LANES:(t + 1) * V7X_LANES], cos_t, sin_t, odd16)
        blk_t = (blk * (ATT_SCALE * LOG2_E)).T.astype(BF16)
        for u in range(heads_per_blk):
            gi, ri = divmod(t * heads_per_blk + u, ATT_REP)
            for s in range(tm // Q_POS):
                col = s * ATT_REP * Q_POS + ri * Q_POS
                q_ref[0, gi, :, col:col + Q_POS] = blk_t[u * ATT_HEAD_DIM:(u + 1) * ATT_HEAD_DIM,
                                                         s * Q_POS:(s + 1) * Q_POS]
    k = proj[:, OFF_K:OFF_V]
    kn = k * lax.rsqrt(_split_dot(k * k, j_ref[:ATT_KV_WIDTH, :ATT_KV_WIDTH]) * inv_hd + EPS) * kg_ref[0]
    k_ref[0] = _rope128(kn, cos_t, sin_t, odd16).astype(BF16)
    v_ref[0] = proj[:, OFF_V:OFF_CB].T.astype(BF16)

    u_act = _gelu_tanh(proj[:, OFF_AU:OFF_AV])
    v_act = _gelu_tanh(proj[:, OFF_AV:OFF_Q])
    jl = j_ref[:A_WIDTH, :A_WIDTH]
    mu = _split_dot(v_act, jl) * (1.0 / A_HEAD_DIM)
    xc = v_act - mu
    var = _split_dot(xc * xc, jl) * (1.0 / A_HEAD_DIM)
    vln = xc * lax.rsqrt(var + EPS)
    lane_a = lax.broadcasted_iota(jnp.int32, (CHUNK, A_WIDTH), 1)
    for c in range(tm // CHUNK):
        vc = vln[c * CHUNK:(c + 1) * CHUNK]
        vs = jnp.concatenate(
            [jnp.where((lane_a // A_HEAD_DIM) == hh, vc, 0.0).astype(BF16) for hh in range(A_HEADS)],
            axis=0)
        mixed = jnp.dot(gw_ref[0], vs, preferred_element_type=F32) + gb_ref[0]
        a_ref[0, c * CHUNK:(c + 1) * CHUNK, :] = (u_act[c * CHUNK:(c + 1) * CHUNK] * mixed).astype(BF16)

    ch = pch[:, :C_WIDTH] * pch[:, C_WIDTH:]
    cm = proj[:, OFF_CB:OFF_CC] * _conv3_seq(ch, cw_ref[0], tm)
    cm_ref[0] = cm.astype(BF16)


def _row_tile_specs(tm, n, d):
    nb8 = n // HALO
    r = tm // HALO
    return [
        pl.BlockSpec((1, tm, d), lambda b, i: (b, i, 0)),
        pl.BlockSpec((1, HALO, d), lambda b, i: (b, jnp.maximum(i * r - 1, 0), 0)),
        pl.BlockSpec((1, HALO, d), lambda b, i: (b, jnp.minimum((i + 1) * r, nb8 - 1), 0)),
    ]


def _const_spec(shape):
    nd = len(shape)
    return pl.BlockSpec(shape, lambda b, i: (0,) * nd, pipeline_mode=pl.Buffered(1))


def _inproj_call(x, l, mod, mod_row, g1, w_in, qg, kg, cos_t, sin_t, jmat, gw, gb, cw, tm):
    bsz, n, d = x.shape
    kern = functools.partial(_inproj_kernel, tm=tm)
    row = lambda w: pl.BlockSpec((1, tm, w), lambda b, i: (b, i, 0))
    est = (4 * tm * d * 4 + d * D_IN * 2 + 8 * tm * D_IN * 4 + 4 * tm * 2048 * 2)
    return pl.pallas_call(
        kern,
        grid=(bsz, n // tm),
        in_specs=_row_tile_specs(tm, n, d) + [
            _mod_spec(mod, l, mod_row),
            _layer_spec(g1, l),
            _layer_spec(w_in, l),
            _layer_spec(qg, l),
            _layer_spec(kg, l),
            pl.BlockSpec((tm, V7X_LANES), lambda b, i: (i, 0)),
            pl.BlockSpec((tm, V7X_LANES), lambda b, i: (i, 0)),
            _const_spec(jmat.shape),
            _layer_spec(gw, l),
            _layer_spec(gb, l),
            _layer_spec(cw, l),
        ],
        out_specs=[
            pl.BlockSpec((1, ATT_KV_HEADS, ATT_HEAD_DIM, ATT_REP * tm), lambda b, i: (b, 0, 0, i)),
            row(ATT_KV_WIDTH),
            pl.BlockSpec((1, ATT_KV_WIDTH, tm), lambda b, i: (b, 0, i)),
            row(A_WIDTH), row(C_WIDTH)],
        out_shape=[
            jax.ShapeDtypeStruct((bsz, ATT_KV_HEADS, ATT_HEAD_DIM, ATT_REP * n), BF16),
            jax.ShapeDtypeStruct((bsz, n, ATT_KV_WIDTH), BF16),
            jax.ShapeDtypeStruct((bsz, ATT_KV_WIDTH, n), BF16),
            jax.ShapeDtypeStruct((bsz, n, A_WIDTH), BF16),
            jax.ShapeDtypeStruct((bsz, n, C_WIDTH), BF16),
        ],
        compiler_params=pltpu.CompilerParams(
            dimension_semantics=("arbitrary", "arbitrary"),
            vmem_limit_bytes=_vmem_limit(est)),
        name="inproj",
    )(x, x, x, mod, g1, w_in, qg, kg, cos_t, sin_t, jmat, gw, gb, cw)


def _attn_operands(qt_ref, kv_refs, tk, t):
    g = pl.program_id(1)
    tq = ATT_REP * Q_POS
    qt = qt_ref[0, 0, :, t * tq:(t + 1) * tq]
    zero = jnp.zeros_like(qt)
    qpad = jnp.concatenate([jnp.where(g == gg, qt, zero) for gg in range(ATT_KV_HEADS)], axis=0)
    ones_rows = (lax.broadcasted_iota(jnp.int32, (BF16_ROWS, tk), 0) == 0).astype(BF16)
    k_refs, vt_refs = kv_refs[0::2], kv_refs[1::2]
    starts = [0]
    for kr in k_refs:
        starts.append(starts[-1] + kr.shape[1])

    def pieces(j):
        lo, hi = j * tk, (j + 1) * tk
        return [(s, max(lo, starts[s]) - starts[s], min(hi, starts[s + 1]) - starts[s])
                for s in range(len(k_refs)) if max(lo, starts[s]) < min(hi, starts[s + 1])]

    def cat(parts, axis):
        return parts[0] if len(parts) == 1 else jnp.concatenate(parts, axis=axis)

    def scores(j):
        kb = cat([k_refs[s][0, a:b, :] for s, a, b in pieces(j)], 0)
        return jnp.dot(kb, qpad, preferred_element_type=F32)

    def values(j, with_ones):
        vb = cat([vt_refs[s][0, :, a:b] for s, a, b in pieces(j)], 1)
        return jnp.concatenate([vb, ones_rows], axis=0) if with_ones else vb

    return scores, values, starts[-1] // tk


def _attn_finish(num, den, o_ref, t):
    o = num / den
    heads_per_blk = V7X_LANES // ATT_HEAD_DIM
    for p in range(ATT_REP // heads_per_blk):
        blk = jnp.concatenate(
            [o[:, (p * heads_per_blk + u) * Q_POS:(p * heads_per_blk + u + 1) * Q_POS]
             for u in range(heads_per_blk)], axis=0)
        o_ref[0, t * Q_POS:(t + 1) * Q_POS, p * V7X_LANES:(p + 1) * V7X_LANES] = blk.T.astype(o_ref.dtype)


def _attn_bounded_kernel(qt_ref, *refs, tk):
    *kv_refs, o_ref = refs
    tq = ATT_REP * Q_POS
    tiles = [_attn_operands(qt_ref, kv_refs, tk, t) for t in range(qt_ref.shape[3] // tq)]
    nkb = tiles[0][2]
    steps = [(t, j) for t in range(len(tiles)) for j in range(nkb)]

    def probs(step):
        t, j = step
        p = jnp.exp2(tiles[t][0](j))
        return p.astype(BF16), jnp.sum(p.reshape(tk // V7X_SUBLANES, V7X_SUBLANES, tq), axis=0)

    acc = den = None
    ahead = [probs(s) for s in steps[:ATTN_LOOKAHEAD]]
    for idx, (t, j) in enumerate(steps):
        if idx + ATTN_LOOKAHEAD < len(steps):
            ahead.append(probs(steps[idx + ATTN_LOOKAHEAD]))
        p, psum = ahead.pop(0)
        part = jnp.dot(tiles[t][1](j, False), p, preferred_element_type=F32)
        acc = part if j == 0 else acc + part
        den = psum if j == 0 else den + psum
        if j == nkb - 1:
            _attn_finish(acc, jnp.sum(den, axis=0, keepdims=True), o_ref, t)


def _attn_online_kernel(qt_ref, *refs, tk):
    *kv_refs, o_ref = refs
    tq = ATT_REP * Q_POS
    for t in range(qt_ref.shape[3] // tq):
        scores, values, nkb = _attn_operands(qt_ref, kv_refs, tk, t)
        m = jnp.full((1, tq), NEG_BIG, F32)
        acc = jnp.zeros((SOFTMAX_ROWS, tq), F32)
        for j in range(nkb):
            st = scores(j)
            m_new = jnp.maximum(m, jnp.max(st, axis=0, keepdims=True))
            p = jnp.exp2(st - m_new).astype(BF16)
            acc = jnp.exp2(m - m_new) * acc + jnp.dot(values(j, True), p, preferred_element_type=F32)
            m = m_new
        _attn_finish(acc[:ATT_HEAD_DIM], acc[ATT_HEAD_DIM:ATT_HEAD_DIM + 1], o_ref, t)


def _attn_call(body, name, qt, *kv):
    bsz, g, hd, nq = qt.shape
    nk = sum(k.shape[1] for k in kv[0::2])
    n = nq // ATT_REP
    tiles = max(t for t in range(1, ATTN_TILES + 1) if n % (t * Q_POS) == 0)
    tq = tiles * ATT_REP * Q_POS
    tk = _key_block(nk)
    kern = functools.partial(body, tk=tk)
    est = 4 * nk * V7X_LANES * 2 + 4 * hd * nk * 2 + 10 * tk * ATT_REP * Q_POS * 4
    kv_specs = []
    for k, vt in zip(kv[0::2], kv[1::2]):
        kv_specs.append(pl.BlockSpec((1, k.shape[1], ATT_KV_WIDTH), lambda b, gg, i: (b, 0, 0)))
        kv_specs.append(pl.BlockSpec((1, hd, vt.shape[2]), lambda b, gg, i: (b, gg, 0)))
    return pl.pallas_call(
        kern,
        grid=(bsz, g, nq // tq),
        in_specs=[pl.BlockSpec((1, 1, hd, tq), lambda b, gg, i: (b, gg, 0, i))] + kv_specs,
        out_specs=pl.BlockSpec((1, tiles * Q_POS, ATT_REP * hd), lambda b, gg, i: (b, i, gg)),
        out_shape=jax.ShapeDtypeStruct((bsz, n, ATT_WIDTH), BF16),
        compiler_params=pltpu.CompilerParams(
            dimension_semantics=("arbitrary", "arbitrary", "arbitrary"),
            vmem_limit_bytes=_vmem_limit(est)),
        name=name,
    )(qt, *kv)


def _key_block(nk):
    for tk in (256, 128):
        if nk % tk == 0:
            return tk
    raise ValueError(f"unsupported key count {nk}")


def _attention(qt, kv, bounded):
    return lax.cond(
        bounded,
        functools.partial(_attn_call, _attn_bounded_kernel, "flash_attn_bounded"),
        functools.partial(_attn_call, _attn_online_kernel, "flash_attn_online"),
        qt, *kv)


def _outproj_kernel(x_ref, a_ref, att_ref, cm_ref, mod_ref, w_ref, o_ref):
    mix = (jnp.dot(a_ref[0], w_ref[0, :A_WIDTH, :], preferred_element_type=F32)
           + jnp.dot(att_ref[0], w_ref[0, A_WIDTH:A_WIDTH + ATT_WIDTH, :], preferred_element_type=F32)
           + jnp.dot(cm_ref[0], w_ref[0, A_WIDTH + ATT_WIDTH:, :], preferred_element_type=F32))
    o_ref[0] = x_ref[0] + mod_ref[0, 0][2:3, :] * mix


def _outproj_call(x, a, att, cm, l, mod, mod_row, w_out, tm):
    bsz, n, d = x.shape
    row = lambda w: pl.BlockSpec((1, tm, w), lambda b, i: (b, i, 0))
    est = 6 * tm * d * 4 + d * d * 2 + 4 * tm * d * 2
    return pl.pallas_call(
        _outproj_kernel,
        grid=(bsz, n // tm),
        in_specs=[row(d), row(A_WIDTH), row(ATT_WIDTH), row(C_WIDTH),
                  _mod_spec(mod, l, mod_row),
                  _layer_spec(w_out, l)],
        out_specs=row(d),
        out_shape=jax.ShapeDtypeStruct((bsz, n, d), F32),
        compiler_params=pltpu.CompilerParams(
            dimension_semantics=("arbitrary", "arbitrary"),
            vmem_limit_bytes=_vmem_limit(est)),
        name="outproj",
    )(x, a, att, cm, mod, w_out)


def _ffn_kernel(x_ref, xp_ref, xn_ref, mod_ref, g_ref, wu_ref, wc_ref, wd_ref, fg_ref,
                o_ref, *, tm, nchunk, cf, final_norm):
    mod = mod_ref[0, 0]
    for r0 in range(0, x_ref.shape[1], tm):
        _, h = _halo_norm_modulate_rows(x_ref, xp_ref, xn_ref, g_ref[0], mod[3:4, :], mod[4:5, :], r0, tm)
        up_all = jnp.dot(h, wu_ref[0], preferred_element_type=F32)
        acts = []
        for j in range(nchunk):
            cv = _conv3_seq(up_all[:, j * 2 * cf:(j + 1) * 2 * cf], wc_ref[0, :, j * 2 * cf:(j + 1) * 2 * cf], tm)
            a_ = cv[:, :cf]
            g_ = cv[:, cf:]
            acts.append(((g_ * _sigmoid(g_)) * a_).astype(BF16))
        act = jnp.concatenate(acts, axis=1)
        y = x_ref[0, r0:r0 + tm] + mod[5:6, :] * jnp.dot(act, wd_ref[0], preferred_element_type=F32)
        if final_norm:
            ms = jnp.mean(y * y, axis=-1, keepdims=True)
            y = y * lax.rsqrt(ms + EPS) * fg_ref[...]
        o_ref[0, r0:r0 + tm] = y


def _ffn_call(x, l, mod, mod_row, g2, wu, wc, wd, fg, tm, final_norm):
    bsz, n, d = x.shape
    d_ff = wd.shape[1]
    cf = FFN_CHUNK
    nchunk = d_ff // cf
    kern = functools.partial(_ffn_kernel, tm=tm, nchunk=nchunk, cf=cf, final_norm=final_norm)
    tb = FFN_SUBTILES * tm if n % (FFN_SUBTILES * tm) == 0 else tm
    est = 3 * d * d_ff * 2 + 6 * tb * d * 4 + 4 * tm * d * 4 + (tm + 2 * HALO) * 2 * d_ff * 4 + 4 * tm * d_ff * 2
    return pl.pallas_call(
        kern,
        grid=(bsz, n // tb),
        in_specs=_row_tile_specs(tb, n, d) + [
            _mod_spec(mod, l, mod_row),
            _layer_spec(g2, l),
            _layer_spec(wu, l),
            _layer_spec(wc, l),
            _layer_spec(wd, l),
            _const_spec((1, d)),
        ],
        out_specs=pl.BlockSpec((1, tb, d), lambda b, i: (b, i, 0)),
        out_shape=jax.ShapeDtypeStruct((bsz, n, d), F32),
        compiler_params=pltpu.CompilerParams(
            dimension_semantics=("arbitrary", "arbitrary"),
            vmem_limit_bytes=_vmem_limit(est)),
        name="convffn",
    )(x, x, x, mod, g2, wu, wc, wd, fg)


def _rope_tables(n):
    rows = n // GRID_W
    row = jnp.repeat(jnp.arange(rows), GRID_W).astype(F32)
    col = jnp.tile(jnp.arange(GRID_W), rows).astype(F32)
    inv = ROPE_THETA ** (-2.0 * jnp.arange(ROPE_AXIS_FREQS, dtype=F32) / (2 * ROPE_AXIS_FREQS))
    ar = row[:, None] * inv
    ac = col[:, None] * inv
    cos_h = jnp.concatenate([jnp.cos(ar), jnp.cos(ar), jnp.cos(ac), jnp.cos(ac)], axis=1)
    sin_h = jnp.concatenate([-jnp.sin(ar), jnp.sin(ar), -jnp.sin(ac), jnp.sin(ac)], axis=1)
    return jnp.tile(cos_h, (1, 2)), jnp.tile(sin_h, (1, 2))


def _block_ones(width, block):
    idx = jnp.arange(width) // block
    return (idx[:, None] == idx[None, :]).astype(BF16)


def _interleave_gate(w, d_ff, cf):
    lead = w.shape[:-1]
    a = w[..., :d_ff].reshape(lead + (d_ff // cf, cf))
    g = w[..., d_ff:].reshape(lead + (d_ff // cf, cf))
    return jnp.concatenate([a, g], axis=-1).reshape(lead + (2 * d_ff,))


def kernel(x, c, ctx, c_ctx, w_mod, b_mod, norm1_g, w_in, q_norm_g, k_norm_g, gmlp_w, gmlp_b,
           conv_c_w, w_out, norm2_g, ffn_up, ffn_conv_w, ffn_down, final_g):
    bsz, n, d = x.shape
    n_ctx = ctx.shape[1]
    depth = w_mod.shape[0]
    d_ff = ffn_down.shape[1]
    tm_proj, tm_lat, tm_ctx = 1024, 512, n_ctx
    assert n % tm_proj == 0 and n % GRID_W == 0 and n_ctx % CHUNK == 0 and bsz + 1 <= V7X_SUBLANES

    c_rows = jnp.concatenate(
        [c, c_ctx[None, :], jnp.zeros((V7X_SUBLANES - bsz - 1, d), F32)], axis=0)
    mod = _mod_call(c_rows, w_mod, b_mod).reshape(depth, V7X_SUBLANES, N_MOD, d)
    ctx_row = bsz

    cos_l, sin_l = _rope_tables(n)
    cos_c = jnp.ones((n_ctx, V7X_LANES), F32)
    sin_c = jnp.zeros((n_ctx, V7X_LANES), F32)
    jmat = _block_ones(ATT_WIDTH, ATT_HEAD_DIM)
    final_row = final_g.reshape(1, d)

    w_in_b = _cast_stack(w_in, 512)
    w_out_b = _cast_stack(w_out, 512)
    wd_b = _cast_stack(ffn_down, d_ff // 4)
    wu_b = _cast_gate_stack(ffn_up, FFN_CHUNK, 256)
    wc_i = _interleave_gate(ffn_conv_w, d_ff, FFN_CHUNK)
    g1 = norm1_g.reshape(depth, 1, d)
    g2 = norm2_g.reshape(depth, 1, d)
    qg = jnp.tile(q_norm_g, (1, ATT_Q_HEADS)).reshape(depth, 1, ATT_WIDTH)
    kg = jnp.tile(k_norm_g, (1, ATT_KV_HEADS)).reshape(depth, 1, ATT_KV_WIDTH)
    gw = gmlp_w.transpose(0, 2, 1, 3).reshape(depth, CHUNK, A_HEADS * CHUNK).astype(BF16)
    gb = jnp.repeat(gmlp_b.transpose(0, 2, 1), A_HEAD_DIM, axis=2)
    score_bound = (1.02 * ATT_HEAD_DIM * ATT_SCALE * LOG2_E
                   * jnp.max(jnp.abs(q_norm_g), axis=1) * jnp.max(jnp.abs(k_norm_g), axis=1))
    bounded = score_bound <= SCORE_BOUND_LOG2
    inproj_params = (g1, w_in_b, qg, kg)
    inproj_tail = (jmat, gw, gb, conv_c_w)
    ffn_params = (g2, wu_b, wc_i, wd_b, final_row)

    xc = ctx
    for l in range(depth):
        last = l == depth - 1
        q_c, k_c, v_c, a_c, cm_c = _inproj_call(
            xc, l, mod, ctx_row, *inproj_params, cos_c, sin_c, *inproj_tail, tm=tm_ctx)
        if not last:
            att_c = _attention(q_c, (k_c, v_c), bounded[l])
            xc = _outproj_call(xc, a_c, att_c, cm_c, l, mod, ctx_row, w_out_b, tm_ctx)
            xc = _ffn_call(xc, l, mod, ctx_row, *ffn_params, tm_ctx, False)

        q_l, k_l, v_l, a_l, cm_l = _inproj_call(
            x, l, mod, None, *inproj_params, cos_l, sin_l, *inproj_tail, tm=tm_proj)
        att = _attention(q_l, (k_c, v_c, k_l, v_l), bounded[l])
        x = _outproj_call(x, a_l, att, cm_l, l, mod, None, w_out_b, tm_proj)
        x = _ffn_call(x, l, mod, None, *ffn_params, tm_lat, last)
    return x
```
